```python
import jax, jax.numpy as jnp
from jax import lax
import numpy as np

D_MODEL = 1024
BATCH = 4
SEQ = 4096
DEPTH = 4

CHUNK = 64
LEFT_CHUNKS = 8
BAND = LEFT_CHUNKS + 1
HEAD_DIM = 64
ATTN_WIDTH = D_MODEL // 2
N_HEADS = ATTN_WIDTH // HEAD_DIM
MAX_REL = 2 * CHUNK
POOL_WINDOWS = (2, 4, 8, 16)
POOL_WIDTH = D_MODEL - ATTN_WIDTH
POOL_GROUP = POOL_WIDTH // len(POOL_WINDOWS)
MIX_WIDTH = ATTN_WIDTH + POOL_WIDTH
IN_WIDTH = 3 * ATTN_WIDTH + POOL_WIDTH
N_EXPERTS = 32
TOP_K = 4
D_EXPERT = D_MODEL
SWIGLU_ALPHA = 1.702
SWIGLU_LIMIT = 7.0
MOE_BLOCK = 256
NORM_EPS = 1e-6
MASK_VALUE = -1e30

kernel_name = "hymba_chunk_relbias_pool_moe_adaln"


def rmsnorm(x, g):
    xf = x.astype(jnp.float32)
    y = xf * lax.rsqrt(jnp.mean(xf * xf, axis=-1, keepdims=True) + NORM_EPS)
    return (y * g.astype(jnp.float32)).astype(x.dtype)


def chunk_rel_attention(q, k, v, q_g, k_g, rel_bias):
    B, S, _ = q.shape
    NC = S // CHUNK

    def heads(t):
        return t.reshape(B, NC, CHUNK, N_HEADS, HEAD_DIM)

    qc = rmsnorm(heads(q), q_g)
    kc = rmsnorm(heads(k), k_g)
    vc = heads(v)
    pad = ((0, 0), (LEFT_CHUNKS, 0), (0, 0), (0, 0), (0, 0))

    def band(t):
        tp = jnp.pad(t, pad)
        return jnp.concatenate([tp[:, j:j + NC] for j in range(BAND)], axis=2)

    kb, vb = band(kc), band(vc)
    s = jnp.einsum("bnqhd,bnkhd->bnhqk", qc, kb).astype(jnp.float32) * (HEAD_DIM ** -0.5)
    q_in = jnp.arange(CHUNK)
    k_in = jnp.arange(BAND * CHUNK)
    rel = k_in[None, :] - LEFT_CHUNKS * CHUNK - q_in[:, None]
    bias = rel_bias[:, jnp.clip(rel, -MAX_REL, MAX_REL) + MAX_REL]
    key_chunk = jnp.arange(NC)[:, None] - LEFT_CHUNKS + k_in[None, :] // CHUNK
    valid = key_chunk >= 0
    s = s + bias.astype(jnp.float32)[None, None]
    s = jnp.where(valid[None, :, None, None, :], s, MASK_VALUE)
    p = jax.nn.softmax(s, axis=-1).astype(v.dtype)
    o = jnp.einsum("bnhqk,bnkhd->bnqhd", p, vb)
    return o.reshape(B, S, ATTN_WIDTH)


def multiscale_pool(u, pool_w, pool_b, pool_scale):
    B, S, _ = u.shape
    uf = u.astype(jnp.float32)
    cs0 = jnp.pad(jnp.cumsum(uf, axis=1), ((0, 0), (1, 0), (0, 0)))
    t = jnp.arange(S)
    outs = []
    for gi, w in enumerate(POOL_WINDOWS):
        sl = slice(gi * POOL_GROUP, (gi + 1) * POOL_GROUP)
        csg = cs0[..., sl]
        hi = csg[:, 1:]
        lo = jnp.pad(csg[:, :S + 1 - w], ((0, 0), (w - 1, 0), (0, 0)))
        count = jnp.minimum(t + 1, w).astype(jnp.float32)[None, :, None]
        d = (hi - lo) / count - uf[..., sl]
        outs.append(d.astype(u.dtype) @ pool_w[gi] + pool_b[sl])
    return jnp.concatenate(outs, axis=-1) * pool_scale


def clamped_swiglu(gu):
    glu, lin = jnp.split(gu, 2, axis=-1)
    glu = jnp.minimum(glu, SWIGLU_LIMIT)
    lin = jnp.clip(lin, -SWIGLU_LIMIT, SWIGLU_LIMIT)
    return glu * jax.nn.sigmoid(SWIGLU_ALPHA * glu) * (lin + 1.0)


def moe(h, router_w, router_b, w1, b1, w2, b2):
    B, S, D = h.shape
    N = B * S
    NK = N * TOP_K
    xf = h.reshape(N, D)
    logits = (xf @ router_w + router_b).astype(jnp.float32)
    top_vals, top_idx = lax.top_k(logits, TOP_K)
    gates = jax.nn.softmax(top_vals, axis=-1)
    flat_e = top_idx.reshape(NK)
    flat_tok = jnp.repeat(jnp.arange(N, dtype=jnp.int32), TOP_K)
    flat_g = gates.reshape(NK)
    order = jnp.argsort(flat_e)
    se = flat_e[order]
    counts = jnp.bincount(flat_e, length=N_EXPERTS)
    padded = (counts + MOE_BLOCK - 1) // MOE_BLOCK * MOE_BLOCK
    start = jnp.cumsum(counts) - counts
    pend = jnp.cumsum(padded)
    pstart = pend - padded
    dest = pstart[se] + (jnp.arange(NK) - start[se])
    P = NK + N_EXPERTS * MOE_BLOCK
    tok_buf = jnp.zeros((P,), jnp.int32).at[dest].set(flat_tok[order])
    gate_buf = jnp.zeros((P,), jnp.float32).at[dest].set(flat_g[order])
    n_blocks = P // MOE_BLOCK
    block_e = jnp.minimum(
        jnp.searchsorted(pend, jnp.arange(n_blocks) * MOE_BLOCK, side="right"), N_EXPERTS - 1)

    def expert_block(args):
        tok, e = args
        xb = xf[tok]
        a = clamped_swiglu(xb @ w1[e] + b1[e])
        return a @ w2[e] + b2[e]

    out_buf = lax.map(expert_block, (tok_buf.reshape(n_blocks, MOE_BLOCK), block_e))
    y = jnp.zeros((N, D), jnp.float32).at[tok_buf].add(
        out_buf.reshape(P, D).astype(jnp.float32) * gate_buf[:, None])
    return y.reshape(B, S, D).astype(h.dtype)


def setup_inputs(seed: int = 0) -> dict:
    key = jax.random.key(seed)
    ks = jax.random.split(key, 20)
    L, D, E, F = DEPTH, D_MODEL, N_EXPERTS, D_EXPERT
    G = len(POOL_WINDOWS)

    def nrm(k, shape, s):
        return s * jax.random.normal(k, shape, jnp.float32)

    return {
        "x": nrm(ks[0], (BATCH, SEQ, D), 1.0),
        "c": nrm(ks[1], (BATCH, D), 1.0),
        "norm1_g": 1.0 + nrm(ks[2], (L, D), 0.05),
        "w_mod": nrm(ks[3], (L, D, 6 * D), 0.5 * D ** -0.5),
        "b_mod": nrm(ks[4], (L, 6 * D), 0.02),
        "w_in": nrm(ks[5], (L, D, IN_WIDTH), D ** -0.5),
        "q_norm_g": 1.0 + nrm(ks[6], (L, HEAD_DIM), 0.05),
        "k_norm_g": 1.0 + nrm(ks[7], (L, HEAD_DIM), 0.05),
        "rel_bias": nrm(ks[8], (L, N_HEADS, 2 * MAX_REL + 1), 0.5),
        "pool_w": nrm(ks[9], (L, G, POOL_GROUP, POOL_GROUP), POOL_GROUP ** -0.5),
        "pool_b": nrm(ks[10], (L, POOL_WIDTH), 0.02),
        "pool_scale": 1.0 + nrm(ks[11], (L, POOL_WIDTH), 0.05),
        "w_out": nrm(ks[12], (L, MIX_WIDTH, D), MIX_WIDTH ** -0.5),
        "norm2_g": 1.0 + nrm(ks[13], (L, D), 0.05),
        "router_w": nrm(ks[14], (L, D, E), D ** -0.5),
        "router_b": nrm(ks[15], (L, E), 0.01),
        "exp_w1": nrm(ks[16], (L, E, D, 2 * F), D ** -0.5),
        "exp_b1": nrm(ks[17], (L, E, 2 * F), 0.02),
        "exp_w2": nrm(ks[18], (L, E, F, D), F ** -0.5),
        "exp_b2": nrm(ks[19], (L, E, D), 0.02),
    }


def reference(x, c, norm1_g, w_mod, b_mod, w_in, q_norm_g, k_norm_g, rel_bias, pool_w, pool_b,
              pool_scale, w_out, norm2_g, router_w, router_b, exp_w1, exp_b1, exp_w2, exp_b2):
    cond = jax.nn.silu(c)
    for l in range(DEPTH):
        mod = cond @ w_mod[l] + b_mod[l]
        sh1, sc1, g1, sh2, sc2, g2 = [m[:, None, :] for m in jnp.split(mod, 6, axis=-1)]
        h = rmsnorm(x, norm1_g[l]) * (1.0 + sc1) + sh1
        z = h @ w_in[l]
        q, k, v, u = jnp.split(z, [ATTN_WIDTH, 2 * ATTN_WIDTH, 3 * ATTN_WIDTH], axis=-1)
        attn = chunk_rel_attention(q, k, v, q_norm_g[l], k_norm_g[l], rel_bias[l])
        pooled = multiscale_pool(u, pool_w[l], pool_b[l], pool_scale[l])
        x = x + g1 * (jnp.concatenate([attn, pooled], axis=-1) @ w_out[l])
        h = rmsnorm(x, norm2_g[l]) * (1.0 + sc2) + sh2
        x = x + g2 * moe(h, router_w[l], router_b[l], exp_w1[l], exp_b1[l], exp_w2[l], exp_b2[l])
    return x
```

```python
import functools

import jax
import jax.numpy as jnp
from jax import lax
from jax.experimental import pallas as pl
from jax.experimental.pallas import tpu as pltpu

F32 = jnp.float32
BF16 = jnp.bfloat16

CHUNK = 64
LEFT_CHUNKS = 8
BAND = LEFT_CHUNKS + 1
HEAD_DIM = 64
N_HEADS = 8
ATTN_WIDTH = N_HEADS * HEAD_DIM
MAX_REL = 2 * CHUNK
POOL_WINDOWS = (2, 4, 8, 16)
POOL_GROUP = 128
POOL_WIDTH = POOL_GROUP * len(POOL_WINDOWS)
N_EXPERTS = 32
TOP_K = 4
SWIGLU_ALPHA = 1.702
SWIGLU_LIMIT = 7.0
MOE_BLOCK = 256
NORM_EPS = 1e-6
MASK_VALUE = -1e30

LANES = 128
POOL_HALO = 16

IN_ROWS = 512
ATTN_CHUNKS = 4
OUT_ROWS = 256
TOK_ROWS = 256
VMEM_LIMIT = 56 * 1024 * 1024


def _cparams(sem, vmem=VMEM_LIMIT):
    return pltpu.CompilerParams(dimension_semantics=sem, vmem_limit_bytes=vmem)


def _mod_kernel(c_ref, w_ref, b_ref, o_ref):
    c = c_ref[...]
    cond = c * jax.nn.sigmoid(c)
    o_ref[0] = jnp.dot(cond.astype(BF16), w_ref[0].astype(BF16), preferred_element_type=F32) + b_ref[0]


def _modulation(c, w_mod, b_mod):
    L, D, W = w_mod.shape
    B = c.shape[0]
    tn = 1024
    return pl.pallas_call(
        _mod_kernel,
        grid=(L, W // tn),
        in_specs=[
            pl.BlockSpec((B, D), lambda l, j: (0, 0)),
            pl.BlockSpec((1, D, tn), lambda l, j: (l, 0, j)),
            pl.BlockSpec((1, 1, tn), lambda l, j: (l, 0, j)),
        ],
        out_specs=pl.BlockSpec((1, B, tn), lambda l, j: (l, 0, j)),
        out_shape=jax.ShapeDtypeStruct((L, B, W), F32),
        compiler_params=_cparams(("arbitrary", "arbitrary")),
        name="modulation",
    )(c, w_mod, b_mod.reshape(L, 1, W))


def _head_norm(t, gain):
    rows = t.shape[0]
    lane = lax.broadcasted_iota(jnp.int32, (rows, LANES), 1)
    first = lane < HEAD_DIM
    outs = []
    for p in range(ATTN_WIDTH // LANES):
        blk = t[:, p * LANES:(p + 1) * LANES]
        sq = blk * blk
        sa = jnp.sum(jnp.where(first, sq, 0.0), axis=-1, keepdims=True)
        sb = jnp.sum(jnp.where(first, 0.0, sq), axis=-1, keepdims=True)
        ra = lax.rsqrt(sa * (1.0 / HEAD_DIM) + NORM_EPS)
        rb = lax.rsqrt(sb * (1.0 / HEAD_DIM) + NORM_EPS)
        outs.append(blk * jnp.where(first, ra, rb))
    return jnp.concatenate(outs, axis=-1) * gain


def _mixer_in_kernel(x_ref, sh_ref, sc_ref, g_ref, w_ref, qg_ref, kg_ref,
                     q_ref, k_ref, v_ref, d_ref, halo_ref):
    j = pl.program_id(1)
    rows = x_ref.shape[1]
    x = x_ref[0]
    ms = jnp.mean(x * x, axis=-1, keepdims=True)
    h = (x * lax.rsqrt(ms + NORM_EPS) * g_ref[...]) * (1.0 + sc_ref[0]) + sh_ref[0]
    z = jnp.dot(h.astype(BF16), w_ref[...], preferred_element_type=F32)
    aw = ATTN_WIDTH
    q_ref[0] = (_head_norm(z[:, :aw], qg_ref[...]) * (HEAD_DIM ** -0.5)).astype(q_ref.dtype)
    k_ref[0] = _head_norm(z[:, aw:2 * aw], kg_ref[...]).astype(k_ref.dtype)
    v_ref[0] = z[:, 2 * aw:3 * aw].astype(v_ref.dtype)

    u = z[:, 3 * aw:]

    @pl.when(j == 0)
    def _():
        halo_ref[...] = jnp.zeros_like(halo_ref)

    t_idx = j * rows + lax.broadcasted_iota(jnp.int32, (rows, 1), 0)
    for gi, w in enumerate(POOL_WINDOWS):
        ls = slice(gi * POOL_GROUP, (gi + 1) * POOL_GROUP)
        ug = u[:, ls]
        ext = jnp.concatenate([halo_ref[:, ls], ug], axis=0)
        acc = ext
        span = 1
        while span < w:
            acc = acc + pltpu.roll(acc, span, 0)
            span *= 2
        win = acc[POOL_HALO:, :]
        count = jnp.minimum(t_idx + 1, w).astype(F32)
        d_ref[0, :, ls] = (win / count - ug).astype(d_ref.dtype)
    halo_ref[...] = u[rows - POOL_HALO:, :]


def _mixer_in(x, sh, sc, g, w_in, qg, kg):
    B, S, D = x.shape
    rows = min(IN_ROWS, S)
    wn = w_in.shape[1]
    o_spec = pl.BlockSpec((1, rows, ATTN_WIDTH), lambda b, j: (b, j, 0))
    o_shape = jax.ShapeDtypeStruct((B, S, ATTN_WIDTH), BF16)
    vec = lambda n: pl.BlockSpec((1, n), lambda b, j: (0, 0))
    per_b = pl.BlockSpec((1, 1, D), lambda b, j: (b, 0, 0))
    return pl.pallas_call(
        _mixer_in_kernel,
        grid=(B, S // rows),
        in_specs=[
            pl.BlockSpec((1, rows, D), lambda b, j: (b, j, 0)),
            per_b, per_b, vec(D),
            pl.BlockSpec((D, wn), lambda b, j: (0, 0)),
            vec(ATTN_WIDTH), vec(ATTN_WIDTH),
        ],
        out_specs=[o_spec, o_spec, o_spec, o_spec],
        out_shape=[o_shape, o_shape, o_shape, o_shape],
        scratch_shapes=[pltpu.VMEM((POOL_HALO, POOL_WIDTH), F32)],
        compiler_params=_cparams(("arbitrary", "arbitrary")),
        name="mixer_in",
    )(x, sh, sc, g, w_in, qg, kg)


def _attn_kernel(q_ref, k0_ref, k1_ref, k2_ref, v0_ref, v1_ref, v2_ref, bias_ref, o_ref, kc_ref, vc_ref):
    i = pl.program_id(1)
    tq = q_ref.shape[1]
    ct = tq // CHUNK
    kw = BAND * CHUNK
    for n, (kr, vr) in enumerate(((k0_ref, v0_ref), (k1_ref, v1_ref), (k2_ref, v2_ref))):
        kc_ref[n * tq:(n + 1) * tq, :] = kr[0]
        vc_ref[n * tq:(n + 1) * tq, :] = vr[0]

    lane = lax.broadcasted_iota(jnp.int32, (CHUNK, LANES), 1)
    first = lane < HEAD_DIM
    kpos = lax.broadcasted_iota(jnp.int32, (2 * CHUNK, kw), 1)

    for p in range(ATTN_WIDTH // LANES):
        ls = slice(p * LANES, (p + 1) * LANES)
        bias = bias_ref[p]

        def body(c, carry, ls=ls, bias=bias):
            r0 = pl.multiple_of(c * CHUNK, CHUNK)
            q = q_ref[0, pl.ds(r0, CHUNK), ls]
            zero = jnp.zeros_like(q)
            q2 = jnp.concatenate([jnp.where(first, q, zero), jnp.where(first, zero, q)], axis=0)
            kk = kc_ref[pl.ds(r0, kw), ls]
            vv = vc_ref[pl.ds(r0, kw), ls]
            s = lax.dot_general(q2, kk, (((1,), (1,)), ((), ())), preferred_element_type=F32) + bias
            first_valid = (LEFT_CHUNKS - (i * ct + c)) * CHUNK
            s = jnp.where(kpos >= first_valid, s, MASK_VALUE)
            m = jnp.max(s, axis=-1, keepdims=True)
            e = jnp.exp(s - m)
            l = jnp.sum(e, axis=-1, keepdims=True)
            pv = jnp.dot(e.astype(BF16), vv, preferred_element_type=F32) / l
            o = jnp.where(first, pv[:CHUNK], pv[CHUNK:])
            o_ref[0, pl.ds(r0, CHUNK), ls] = o.astype(o_ref.dtype)
            return carry

        lax.fori_loop(0, ct, body, 0)


def _attention(q, k, v, bias):
    B, S, W = q.shape
    ct = min(ATTN_CHUNKS, S // CHUNK)
    tq = ct * CHUNK
    assert LEFT_CHUNKS % ct == 0
    nback = LEFT_CHUNKS // ct
    assert nback == 2
    cur = pl.BlockSpec((1, tq, W), lambda b, i: (b, i, 0))
    back = lambda n: pl.BlockSpec((1, tq, W), lambda b, i: (b, jnp.maximum(i - n, 0), 0))
    return pl.pallas_call(
        _attn_kernel,
        grid=(B, S // tq),
        in_specs=[cur, back(2), back(1), cur, back(2), back(1), cur,
                  pl.BlockSpec(bias.shape, lambda b, i: (0, 0, 0))],
        out_specs=cur,
        out_shape=jax.ShapeDtypeStruct((B, S, W), BF16),
        scratch_shapes=[pltpu.VMEM(((nback + 1) * tq, W), BF16), pltpu.VMEM(((nback + 1) * tq, W), BF16)],
        compiler_params=_cparams(("arbitrary", "arbitrary")),
        name="attention",
    )(q, k, k, k, v, v, v, bias)


def _attn_bias(rel_bias):
    q_in = jnp.arange(CHUNK)
    k_in = jnp.arange(BAND * CHUNK)
    rel = k_in[None, :] - LEFT_CHUNKS * CHUNK - q_in[:, None]
    bias = rel_bias[:, jnp.clip(rel, -MAX_REL, MAX_REL) + MAX_REL].astype(F32)
    return bias.reshape(N_HEADS // 2, 2 * CHUNK, BAND * CHUNK)


def _mixer_out_kernel(x_ref, a_ref, d_ref, pw_ref, pb_ref, ps_ref, wo_ref, g1_ref, n2_ref, sh_ref, sc_ref,
                      rw_ref, rb_ref, x1_ref, h2_ref, route_ref, cnt_ref, run_ref):
    step = pl.program_id(0) * pl.num_programs(1) + pl.program_id(1)
    rows = x_ref.shape[1]

    @pl.when(step == 0)
    def _():
        run_ref[...] = jnp.zeros_like(run_ref)

    y = jnp.dot(a_ref[0], wo_ref[:ATTN_WIDTH, :], preferred_element_type=F32)
    for gi in range(len(POOL_WINDOWS)):
        ls = slice(gi * POOL_GROUP, (gi + 1) * POOL_GROUP)
        pg = jnp.dot(d_ref[0, :, ls], pw_ref[gi], preferred_element_type=F32)
        pg = (pg + pb_ref[:, ls]) * ps_ref[:, ls]
        y = y + jnp.dot(pg.astype(BF16), wo_ref[ATTN_WIDTH + gi * POOL_GROUP:ATTN_WIDTH + (gi + 1) * POOL_GROUP, :],
                        preferred_element_type=F32)
    x1 = x_ref[0] + g1_ref[0] * y
    x1_ref[0] = x1

    ms = jnp.mean(x1 * x1, axis=-1, keepdims=True)
    h2 = (x1 * lax.rsqrt(ms + NORM_EPS) * n2_ref[...]) * (1.0 + sc_ref[0]) + sh_ref[0]
    h2_ref[0] = h2

    logits = jnp.dot(h2.astype(BF16), rw_ref[...], preferred_element_type=F32) + rb_ref[...]
    ne = logits.shape[-1]
    elane = lax.broadcasted_iota(jnp.int32, (rows, ne), 1)
    work = logits
    picked = jnp.zeros((rows, ne), F32)
    vals, idxs = [], []
    for _ in range(TOP_K):
        m = jnp.max(work, axis=-1, keepdims=True)
        idx = jnp.min(jnp.where(work == m, elane, ne), axis=-1, keepdims=True)
        sel = elane == idx
        vals.append(m)
        idxs.append(idx)
        work = jnp.where(sel, -jnp.inf, work)
        picked = picked + sel.astype(F32)
    exps = [jnp.exp(v - vals[0]) for v in vals]
    denom = exps[0] + exps[1] + exps[2] + exps[3]

    r_i = lax.broadcasted_iota(jnp.int32, (rows, rows), 0)
    c_i = lax.broadcasted_iota(jnp.int32, (rows, rows), 1)
    tri = (c_i < r_i).astype(BF16)
    before = jnp.dot(tri, picked.astype(BF16), preferred_element_type=F32) + run_ref[...]
    run_ref[...] = run_ref[...] + jnp.sum(picked, axis=0, keepdims=True)

    lane = lax.broadcasted_iota(jnp.int32, (rows, LANES), 1)
    route = jnp.zeros((rows, LANES), F32)
    for kk in range(TOP_K):
        rank = jnp.sum(jnp.where(elane == idxs[kk], before, 0.0), axis=-1, keepdims=True)
        route = jnp.where(lane == kk, idxs[kk].astype(F32), route)
        route = jnp.where(lane == TOP_K + kk, exps[kk] / denom, route)
        route = jnp.where(lane == 2 * TOP_K + kk, rank, route)
    route_ref[0] = route
    cnt_ref[...] = jnp.broadcast_to(run_ref[...], cnt_ref.shape)


def _mixer_out(x, attn, d, pool_w, pool_b, pool_scale, w_out, g1, n2, sh2, sc2, router_w, router_b):
    B, S, D = x.shape
    rows = min(OUT_ROWS, S)
    E = router_w.shape[1]
    tile = lambda w: pl.BlockSpec((1, rows, w), lambda b, j: (b, j, 0))
    per_b = pl.BlockSpec((1, 1, D), lambda b, j: (b, 0, 0))
    const2 = lambda a: pl.BlockSpec(a.shape, lambda b, j: (0, 0))
    return pl.pallas_call(
        _mixer_out_kernel,
        grid=(B, S // rows),
        in_specs=[
            tile(D), tile(ATTN_WIDTH), tile(POOL_WIDTH),
            pl.BlockSpec(pool_w.shape, lambda b, j: (0, 0, 0)),
            const2(pool_b), const2(pool_scale), const2(w_out),
            per_b, const2(n2), per_b, per_b, const2(router_w), const2(router_b),
        ],
        out_specs=[tile(D), tile(D), tile(LANES), pl.BlockSpec((8, E), lambda b, j: (0, 0))],
        out_shape=[
            jax.ShapeDtypeStruct((B, S, D), F32),
            jax.ShapeDtypeStruct((B, S, D), F32),
            jax.ShapeDtypeStruct((B, S, LANES), F32),
            jax.ShapeDtypeStruct((8, E), F32),
        ],
        scratch_shapes=[pltpu.VMEM((1, E), F32)],
        compiler_params=_cparams(("arbitrary", "arbitrary")),
        name="mixer_out",
    )(x, attn, d, pool_w, pool_b, pool_scale, w_out, g1, n2, sh2, sc2, router_w, router_b)


def _dispatch_kernel(dest_ref, h_ref, xs_in_ref, xs_ref, sem):
    del xs_in_ref
    rows = h_ref.shape[0]

    def row_copy(r, slot):
        return pltpu.make_async_copy(h_ref.at[pl.ds(r, 1)], xs_ref.at[pl.ds(slot, 1)], sem)

    def issue(r, carry):
        for kk in range(TOP_K):
            row_copy(r, dest_ref[0, 0, r * TOP_K + kk]).start()
        return carry

    lax.fori_loop(0, rows, issue, 0)
    for _ in range(TOP_K):
        pltpu.make_async_copy(h_ref, xs_ref.at[pl.ds(0, rows)], sem).wait()


def _dispatch(h2, dest, n_slots):
    N, D = h2.shape
    rows = min(TOK_ROWS, N)
    steps = N // rows
    dest3 = dest.reshape(steps, 1, rows * TOP_K)
    return pl.pallas_call(
        _dispatch_kernel,
        grid=(steps,),
        in_specs=[
            pl.BlockSpec((1, 1, rows * TOP_K), lambda i: (i, 0, 0), memory_space=pltpu.SMEM),
            pl.BlockSpec((rows, D), lambda i: (i, 0)),
            pl.BlockSpec(memory_space=pl.ANY),
        ],
        out_specs=pl.BlockSpec(memory_space=pl.ANY),
        out_shape=jax.ShapeDtypeStruct((n_slots, D), h2.dtype),
        scratch_shapes=[pltpu.SemaphoreType.DMA(())],
        input_output_aliases={2: 0},
        compiler_params=_cparams(("arbitrary",)),
        name="dispatch",
    )(dest3, h2, jnp.zeros((n_slots, D), h2.dtype))


def _experts_kernel(be_ref, first_ref, nused_ref, xs_ref, w1_ref, b1_ref, w2_ref, b2_ref, o_ref, w1s_ref, w2s_ref):
    i = pl.program_id(0)

    @pl.when(i < nused_ref[0])
    def _():
        @pl.when(first_ref[i] == 1)
        def _():
            w1s_ref[...] = w1_ref[0].astype(BF16)
            w2s_ref[...] = w2_ref[0].astype(BF16)

        f = w2_ref.shape[1]
        gu = jnp.dot(xs_ref[...].astype(BF16), w1s_ref[...], preferred_element_type=F32) + b1_ref[0]
        glu = jnp.minimum(gu[:, :f], SWIGLU_LIMIT)
        lin = jnp.clip(gu[:, f:], -SWIGLU_LIMIT, SWIGLU_LIMIT)
        act = glu * jax.nn.sigmoid(SWIGLU_ALPHA * glu) * (lin + 1.0)
        o_ref[...] = jnp.dot(act.astype(BF16), w2s_ref[...], preferred_element_type=F32) + b2_ref[0]

    @pl.when(i >= nused_ref[0])
    def _():
        o_ref[...] = jnp.zeros_like(o_ref)


def _experts(xs, w1, b1, w2, b2, block_e, first, nused):
    P, D = xs.shape
    E, _, F2 = w1.shape
    F = w2.shape[1]
    nb = P // MOE_BLOCK
    row_blk = lambda i, be, fi, nu: (jnp.minimum(i, nu[0] - 1), 0)
    by_e = lambda i, be, fi, nu: (be[i], 0, 0)
    grid_spec = pltpu.PrefetchScalarGridSpec(
        num_scalar_prefetch=3,
        grid=(nb,),
        in_specs=[
            pl.BlockSpec((MOE_BLOCK, D), row_blk),
            pl.BlockSpec((1, D, F2), by_e),
            pl.BlockSpec((1, 1, F2), by_e),
            pl.BlockSpec((1, F, D), by_e),
            pl.BlockSpec((1, 1, D), by_e),
        ],
        out_specs=pl.BlockSpec((MOE_BLOCK, D), lambda i, be, fi, nu: (i, 0)),
        scratch_shapes=[pltpu.VMEM((D, F2), BF16), pltpu.VMEM((F, D), BF16)],
    )
    return pl.pallas_call(
        _experts_kernel,
        grid_spec=grid_spec,
        out_shape=jax.ShapeDtypeStruct((P, D), F32),
        compiler_params=_cparams(("arbitrary",)),
        name="experts",
    )(block_e, first, nused, xs, w1, b1.reshape(E, 1, F2), w2, b2.reshape(E, 1, D))


def _combine_kernel(dest_ref, ob_ref, route_ref, x_ref, g2_ref, o_ref, buf_ref, sem):
    rows = x_ref.shape[0]

    def issue(r, carry):
        for kk in range(TOP_K):
            slot = dest_ref[0, 0, r * TOP_K + kk]
            pltpu.make_async_copy(ob_ref.at[pl.ds(slot, 1)], buf_ref.at[kk, pl.ds(r, 1)], sem).start()
        return carry

    lax.fori_loop(0, rows, issue, 0)
    for kk in range(TOP_K):
        pltpu.make_async_copy(ob_ref.at[pl.ds(0, rows)], buf_ref.at[kk], sem).wait()

    route = route_ref[...]
    y = jnp.zeros(x_ref.shape, F32)
    for kk in range(TOP_K):
        y = y + route[:, TOP_K + kk:TOP_K + kk + 1] * buf_ref[kk]
    o_ref[...] = x_ref[...] + g2_ref[0] * y


def _combine(out_buf, dest, route, x1, g2, seq):
    N, D = x1.shape
    rows = min(TOK_ROWS, N)
    steps = N // rows
    per_seq = seq // rows
    dest3 = dest.reshape(steps, 1, rows * TOP_K)
    return pl.pallas_call(
        _combine_kernel,
        grid=(steps,),
        in_specs=[
            pl.BlockSpec((1, 1, rows * TOP_K), lambda i: (i, 0, 0), memory_space=pltpu.SMEM),
            pl.BlockSpec(memory_space=pl.ANY),
            pl.BlockSpec((rows, LANES), lambda i: (i, 0)),
            pl.BlockSpec((rows, D), lambda i: (i, 0)),
            pl.BlockSpec((1, 1, D), lambda i: (i // per_seq, 0, 0)),
        ],
        out_specs=pl.BlockSpec((rows, D), lambda i: (i, 0)),
        out_shape=jax.ShapeDtypeStruct((N, D), F32),
        scratch_shapes=[pltpu.VMEM((TOP_K, rows, D), F32), pltpu.SemaphoreType.DMA(())],
        compiler_params=_cparams(("arbitrary",)),
        name="combine",
    )(dest3, out_buf, route, x1, g2)


def _slots(route, counts, n_tokens):
    idx = route[:, :TOP_K].astype(jnp.int32)
    rank = route[:, 2 * TOP_K:3 * TOP_K].astype(jnp.int32)
    counts = counts.astype(jnp.int32)
    padded = (counts + MOE_BLOCK - 1) // MOE_BLOCK * MOE_BLOCK
    pend = jnp.cumsum(padded)
    pstart = pend - padded
    dest = pstart[idx] + rank
    n_slots = n_tokens * TOP_K + N_EXPERTS * MOE_BLOCK
    n_blocks = n_slots // MOE_BLOCK
    nused = (pend[-1] // MOE_BLOCK).astype(jnp.int32)
    blk = jnp.minimum(jnp.arange(n_blocks, dtype=jnp.int32), nused - 1)
    block_e = jnp.minimum(jnp.searchsorted(pend, blk * MOE_BLOCK, side="right"), N_EXPERTS - 1).astype(jnp.int32)
    first = jnp.concatenate([jnp.ones((1,), jnp.int32), (block_e[1:] != block_e[:-1]).astype(jnp.int32)])
    return dest.astype(jnp.int32), block_e, first, nused.reshape(1), n_slots


def kernel(x, c, norm1_g, w_mod, b_mod, w_in, q_norm_g, k_norm_g, rel_bias, pool_w, pool_b, pool_scale, w_out,
           norm2_g, router_w, router_b, exp_w1, exp_b1, exp_w2, exp_b2):
    B, S, D = x.shape
    L = w_mod.shape[0]
    N = B * S
    mod = _modulation(c, w_mod, b_mod)
    for l in range(L):
        sh1, sc1, g1, sh2, sc2, g2 = [mod[l, :, i * D:(i + 1) * D].reshape(B, 1, D) for i in range(6)]
        tile_heads = lambda g: jnp.tile(g, N_HEADS).reshape(1, ATTN_WIDTH)
        q, k, v, d = _mixer_in(x, sh1, sc1, norm1_g[l].reshape(1, D), w_in[l].astype(BF16),
                               tile_heads(q_norm_g[l]), tile_heads(k_norm_g[l]))
        attn = _attention(q, k, v, _attn_bias(rel_bias[l]))
        x1, h2, route, counts = _mixer_out(
            x, attn, d, pool_w[l].astype(BF16), pool_b[l].reshape(1, -1), pool_scale[l].reshape(1, -1),
            w_out[l].astype(BF16), g1, norm2_g[l].reshape(1, D), sh2, sc2,
            router_w[l].astype(BF16), router_b[l].reshape(1, -1))
        route = route.reshape(N, LANES)
        dest, block_e, first, nused, n_slots = _slots(route, counts[0], N)
        xs = _dispatch(h2.reshape(N, D), dest, n_slots)
        out_buf = _experts(xs, exp_w1[l], exp_b1[l], exp_w2[l], exp_b2[l], block_e, first, nused)
        x = _combine(out_buf, dest, route, x1.reshape(N, D), g2, S).reshape(B, S, D)
    return x
```

```python
import functools

import jax
import jax.numpy as jnp
from jax import lax
from jax.experimental import pallas as pl
from jax.experimental.pallas import tpu as pltpu

F32 = jnp.float32
BF16 = jnp.bfloat16

CHUNK = 64
LEFT_CHUNKS = 8
BAND = LEFT_CHUNKS + 1
HEAD_DIM = 64
N_HEADS = 8
ATTN_WIDTH = N_HEADS * HEAD_DIM
MAX_REL = 2 * CHUNK
POOL_WINDOWS = (2, 4, 8, 16)
POOL_GROUP = 128
POOL_WIDTH = POOL_GROUP * len(POOL_WINDOWS)
N_EXPERTS = 32
TOP_K = 4
SWIGLU_ALPHA = 1.702
SWIGLU_LIMIT = 7.0
MOE_BLOCK = 256
NORM_EPS = 1e-6
MASK_VALUE = -1e30

LANES = 128
POOL_HALO = 16

IN_ROWS = 512
ATTN_CHUNKS = 4
OUT_ROWS = 256
TOK_ROWS = 256
VMEM_LIMIT = 56 * 1024 * 1024


def _cparams(sem, vmem=VMEM_LIMIT):
    return pltpu.CompilerParams(dimension_semantics=sem, vmem_limit_bytes=vmem)


def _layer_spec(a, l):
    zeros = (0,) * (a.ndim - 1)
    return pl.BlockSpec((1,) + a.shape[1:], lambda *_: (l,) + zeros)


def _mod_kernel(c_ref, w_ref, b_ref, o_ref):
    c = c_ref[...]
    cond = c * jax.nn.sigmoid(c)
    o_ref[0] = jnp.dot(cond.astype(BF16), w_ref[0].astype(BF16), preferred_element_type=F32) + b_ref[0]


def _modulation(c, w_mod, b_mod):
    L, D, W = w_mod.shape
    B = c.shape[0]
    tn = 1024
    return pl.pallas_call(
        _mod_kernel,
        grid=(L, W // tn),
        in_specs=[
            pl.BlockSpec((B, D), lambda l, j: (0, 0)),
            pl.BlockSpec((1, D, tn), lambda l, j: (l, 0, j)),
            pl.BlockSpec((1, 1, tn), lambda l, j: (l, 0, j)),
        ],
        out_specs=pl.BlockSpec((1, B, tn), lambda l, j: (l, 0, j)),
        out_shape=jax.ShapeDtypeStruct((L, B, W), F32),
        compiler_params=_cparams(("arbitrary", "arbitrary")),
        name="modulation",
    )(c, w_mod, b_mod.reshape(L, 1, W))


def _head_norm(t, gain):
    rows = t.shape[0]
    lane = lax.broadcasted_iota(jnp.int32, (rows, LANES), 1)
    first = lane < HEAD_DIM
    outs = []
    for p in range(ATTN_WIDTH // LANES):
        blk = t[:, p * LANES:(p + 1) * LANES]
        sq = blk * blk
        sa = jnp.sum(jnp.where(first, sq, 0.0), axis=-1, keepdims=True)
        sb = jnp.sum(jnp.where(first, 0.0, sq), axis=-1, keepdims=True)
        ra = lax.rsqrt(sa * (1.0 / HEAD_DIM) + NORM_EPS)
        rb = lax.rsqrt(sb * (1.0 / HEAD_DIM) + NORM_EPS)
        outs.append(blk * jnp.where(first, ra, rb))
    return jnp.concatenate(outs, axis=-1) * gain


def _mixer_in_kernel(x_ref, sh_ref, sc_ref, g_ref, w_ref, qg_ref, kg_ref,
                     q_ref, k_ref, v_ref, d_ref, halo_ref):
    j = pl.program_id(1)
    rows = x_ref.shape[1]
    x = x_ref[0]
    ms = jnp.mean(x * x, axis=-1, keepdims=True)
    h = (x * lax.rsqrt(ms + NORM_EPS) * g_ref[0]) * (1.0 + sc_ref[0]) + sh_ref[0]
    z = jnp.dot(h.astype(BF16), w_ref[0], preferred_element_type=F32)
    aw = ATTN_WIDTH
    q_ref[0] = (_head_norm(z[:, :aw], qg_ref[0]) * (HEAD_DIM ** -0.5)).astype(q_ref.dtype)
    k_ref[0] = _head_norm(z[:, aw:2 * aw], kg_ref[0]).astype(k_ref.dtype)
    v_ref[0] = z[:, 2 * aw:3 * aw].astype(v_ref.dtype)

    u = z[:, 3 * aw:]

    @pl.when(j == 0)
    def _():
        halo_ref[...] = jnp.zeros_like(halo_ref)

    t_idx = j * rows + lax.broadcasted_iota(jnp.int32, (rows, 1), 0)
    for gi, w in enumerate(POOL_WINDOWS):
        ls = slice(gi * POOL_GROUP, (gi + 1) * POOL_GROUP)
        ug = u[:, ls]
        ext = jnp.concatenate([halo_ref[:, ls], ug], axis=0)
        acc = ext
        span = 1
        while span < w:
            acc = acc + pltpu.roll(acc, span, 0)
            span *= 2
        win = acc[POOL_HALO:, :]
        count = jnp.minimum(t_idx + 1, w).astype(F32)
        d_ref[0, :, ls] = (win / count - ug).astype(d_ref.dtype)
    halo_ref[...] = u[rows - POOL_HALO:, :]


def _mixer_in(l, x, sh, sc, g, w_in, qg, kg):
    B, S, D = x.shape
    rows = min(IN_ROWS, S)
    o_spec = pl.BlockSpec((1, rows, ATTN_WIDTH), lambda b, j: (b, j, 0))
    o_shape = jax.ShapeDtypeStruct((B, S, ATTN_WIDTH), BF16)
    per_b = pl.BlockSpec((1, 1, D), lambda b, j: (b, 0, 0))
    return pl.pallas_call(
        _mixer_in_kernel,
        grid=(B, S // rows),
        in_specs=[
            pl.BlockSpec((1, rows, D), lambda b, j: (b, j, 0)),
            per_b, per_b, _layer_spec(g, l), _layer_spec(w_in, l), _layer_spec(qg, l), _layer_spec(kg, l),
        ],
        out_specs=[o_spec, o_spec, o_spec, o_spec],
        out_shape=[o_shape, o_shape, o_shape, o_shape],
        scratch_shapes=[pltpu.VMEM((POOL_HALO, POOL_WIDTH), F32)],
        compiler_params=_cparams(("arbitrary", "arbitrary")),
        name="mixer_in",
    )(x, sh, sc, g, w_in, qg, kg)


def _attn_kernel(q_ref, k0_ref, k1_ref, k2_ref, v0_ref, v1_ref, v2_ref, bias_ref, o_ref):
    i = pl.program_id(1)
    tq = q_ref.shape[1]
    k_refs = (k0_ref, k1_ref, k2_ref)
    v_refs = (v0_ref, v1_ref, v2_ref)
    nt = len(k_refs)
    lane = lax.broadcasted_iota(jnp.int32, (tq, LANES), 1)
    first = lane < HEAD_DIM
    kpos = lax.broadcasted_iota(jnp.int32, (2 * tq, nt * tq), 1)
    in_seq = kpos >= (nt - 1 - i) * tq
    nt_dims = (((1,), (1,)), ((), ()))
    for p in range(ATTN_WIDTH // LANES):
        ls = slice(p * LANES, (p + 1) * LANES)
        q = q_ref[0, :, ls]
        zero = jnp.zeros_like(q)
        q2 = jnp.concatenate([jnp.where(first, q, zero), jnp.where(first, zero, q)], axis=0)
        s = jnp.concatenate(
            [lax.dot_general(q2, kr[0, :, ls], nt_dims, preferred_element_type=F32) for kr in k_refs], axis=1)
        s = jnp.where(in_seq, s + bias_ref[p], MASK_VALUE)
        m = jnp.max(s, axis=-1, keepdims=True)
        e = jnp.exp(s - m)
        l = jnp.sum(e, axis=-1, keepdims=True)
        eb = e.astype(BF16)
        pv = jnp.dot(eb[:, :tq], v_refs[0][0, :, ls], preferred_element_type=F32)
        for n in range(1, nt):
            pv = pv + jnp.dot(eb[:, n * tq:(n + 1) * tq], v_refs[n][0, :, ls], preferred_element_type=F32)
        pv = pv / l
        o_ref[0, :, ls] = jnp.where(first, pv[:tq], pv[tq:]).astype(o_ref.dtype)


def _attention(q, k, v, bias):
    B, S, W = q.shape
    tq = ATTN_CHUNKS * CHUNK
    assert LEFT_CHUNKS == 2 * ATTN_CHUNKS and S % tq == 0
    cur = pl.BlockSpec((1, tq, W), lambda b, i: (b, i, 0))
    back = lambda n: pl.BlockSpec((1, tq, W), lambda b, i: (b, jnp.maximum(i - n, 0), 0))
    return pl.pallas_call(
        _attn_kernel,
        grid=(B, S // tq),
        in_specs=[cur, back(2), back(1), cur, back(2), back(1), cur,
                  pl.BlockSpec(bias.shape, lambda b, i: (0, 0, 0))],
        out_specs=cur,
        out_shape=jax.ShapeDtypeStruct((B, S, W), BF16),
        compiler_params=_cparams(("arbitrary", "arbitrary")),
        name="attention",
    )(q, k, k, k, v, v, v, bias)


def _attn_bias(rel_bias):
    tq = ATTN_CHUNKS * CHUNK
    row = jnp.arange(tq)
    col = jnp.arange(3 * tq)
    in_band = col[None, :] - (row[:, None] // CHUNK) * CHUNK
    rel = in_band - LEFT_CHUNKS * CHUNK - (row[:, None] % CHUNK)
    table = rel_bias[:, jnp.clip(rel, -MAX_REL, MAX_REL) + MAX_REL].astype(F32)
    ok = (in_band >= 0) & (in_band < BAND * CHUNK)
    bias = jnp.where(ok[None], table, MASK_VALUE)
    return bias.reshape(N_HEADS // 2, 2 * tq, 3 * tq)


def _mixer_out_kernel(x_ref, a_ref, d_ref, pw_ref, pb_ref, ps_ref, wo_ref, g1_ref, n2_ref, sh_ref, sc_ref,
                      rw_ref, rb_ref, x1_ref, h2_ref, route_ref, cnt_ref, run_ref):
    step = pl.program_id(0) * pl.num_programs(1) + pl.program_id(1)
    rows = x_ref.shape[1]

    @pl.when(step == 0)
    def _():
        run_ref[...] = jnp.zeros_like(run_ref)

    y = jnp.dot(a_ref[0], wo_ref[0, :ATTN_WIDTH, :], preferred_element_type=F32)
    for gi in range(len(POOL_WINDOWS)):
        ls = slice(gi * POOL_GROUP, (gi + 1) * POOL_GROUP)
        pg = jnp.dot(d_ref[0, :, ls], pw_ref[0, gi], preferred_element_type=F32)
        pg = (pg + pb_ref[0, :, ls]) * ps_ref[0, :, ls]
        lo = ATTN_WIDTH + gi * POOL_GROUP
        y = y + jnp.dot(pg.astype(BF16), wo_ref[0, lo:lo + POOL_GROUP, :], preferred_element_type=F32)
    x1 = x_ref[0] + g1_ref[0] * y
    x1_ref[0] = x1

    ms = jnp.mean(x1 * x1, axis=-1, keepdims=True)
    h2 = (x1 * lax.rsqrt(ms + NORM_EPS) * n2_ref[0]) * (1.0 + sc_ref[0]) + sh_ref[0]
    h2_ref[0] = h2

    logits = jnp.dot(h2.astype(BF16), rw_ref[0], preferred_element_type=F32) + rb_ref[0]
    ne = logits.shape[-1]
    elane = lax.broadcasted_iota(jnp.int32, (rows, ne), 1)
    work = logits
    picked = jnp.zeros((rows, ne), F32)
    vals, idxs = [], []
    for _ in range(TOP_K):
        m = jnp.max(work, axis=-1, keepdims=True)
        idx = jnp.min(jnp.where(work == m, elane, ne), axis=-1, keepdims=True)
        sel = elane == idx
        vals.append(m)
        idxs.append(idx)
        work = jnp.where(sel, -jnp.inf, work)
        picked = picked + sel.astype(F32)
    exps = [jnp.exp(v - vals[0]) for v in vals]
    denom = exps[0] + exps[1] + exps[2] + exps[3]

    r_i = lax.broadcasted_iota(jnp.int32, (rows, rows), 0)
    c_i = lax.broadcasted_iota(jnp.int32, (rows, rows), 1)
    tri = (c_i < r_i).astype(BF16)
    before = jnp.dot(tri, picked.astype(BF16), preferred_element_type=F32) + run_ref[...]
    run_ref[...] = run_ref[...] + jnp.sum(picked, axis=0, keepdims=True)

    lane = lax.broadcasted_iota(jnp.int32, (rows, LANES), 1)
    route = jnp.zeros((rows, LANES), F32)
    for kk in range(TOP_K):
        rank = jnp.sum(jnp.where(elane == idxs[kk], before, 0.0), axis=-1, keepdims=True)
        route = jnp.where(lane == kk, idxs[kk].astype(F32), route)
        route = jnp.where(lane == TOP_K + kk, exps[kk] / denom, route)
        route = jnp.where(lane == 2 * TOP_K + kk, rank, route)
    route_ref[0] = route
    cnt_ref[...] = jnp.broadcast_to(run_ref[...], cnt_ref.shape)


def _mixer_out(l, x, attn, d, pool_w, pool_b, pool_scale, w_out, g1, n2, sh2, sc2, router_w, router_b):
    B, S, D = x.shape
    rows = min(OUT_ROWS, S)
    E = router_w.shape[-1]
    tile = lambda w: pl.BlockSpec((1, rows, w), lambda b, j: (b, j, 0))
    per_b = pl.BlockSpec((1, 1, D), lambda b, j: (b, 0, 0))
    lay = lambda a: _layer_spec(a, l)
    return pl.pallas_call(
        _mixer_out_kernel,
        grid=(B, S // rows),
        in_specs=[
            tile(D), tile(ATTN_WIDTH), tile(POOL_WIDTH),
            lay(pool_w), lay(pool_b), lay(pool_scale), lay(w_out),
            per_b, lay(n2), per_b, per_b, lay(router_w), lay(router_b),
        ],
        out_specs=[tile(D), tile(D), tile(LANES), pl.BlockSpec((8, E), lambda b, j: (0, 0))],
        out_shape=[
            jax.ShapeDtypeStruct((B, S, D), F32),
            jax.ShapeDtypeStruct((B, S, D), F32),
            jax.ShapeDtypeStruct((B, S, LANES), F32),
            jax.ShapeDtypeStruct((8, E), F32),
        ],
        scratch_shapes=[pltpu.VMEM((1, E), F32)],
        compiler_params=_cparams(("arbitrary", "arbitrary")),
        name="mixer_out",
    )(x, attn, d, pool_w, pool_b, pool_scale, w_out, g1, n2, sh2, sc2, router_w, router_b)


def _dispatch_kernel(dest_ref, h_ref, xs_in_ref, xs_ref, sem):
    del xs_in_ref
    rows = h_ref.shape[0]

    def row_copy(r, slot):
        return pltpu.make_async_copy(h_ref.at[pl.ds(r, 1)], xs_ref.at[pl.ds(slot, 1)], sem)

    def issue(r, carry):
        for kk in range(TOP_K):
            row_copy(r, dest_ref[0, 0, r * TOP_K + kk]).start()
        return carry

    lax.fori_loop(0, rows, issue, 0)
    for _ in range(TOP_K):
        pltpu.make_async_copy(h_ref, xs_ref.at[pl.ds(0, rows)], sem).wait()


def _dispatch(h2, dest, n_slots):
    N, D = h2.shape
    rows = min(TOK_ROWS, N)
    steps = N // rows
    dest3 = dest.reshape(steps, 1, rows * TOP_K)
    return pl.pallas_call(
        _dispatch_kernel,
        grid=(steps,),
        in_specs=[
            pl.BlockSpec((1, 1, rows * TOP_K), lambda i: (i, 0, 0), memory_space=pltpu.SMEM),
            pl.BlockSpec((rows, D), lambda i: (i, 0)),
            pl.BlockSpec(memory_space=pl.ANY),
        ],
        out_specs=pl.BlockSpec(memory_space=pl.ANY),
        out_shape=jax.ShapeDtypeStruct((n_slots, D), h2.dtype),
        scratch_shapes=[pltpu.SemaphoreType.DMA(())],
        input_output_aliases={2: 0},
        compiler_params=_cparams(("arbitrary",)),
        name="dispatch",
    )(dest3, h2, jnp.zeros((n_slots, D), h2.dtype))


def _experts_kernel(be_ref, first_ref, nused_ref, xs_ref, w1_ref, b1_ref, w2_ref, b2_ref, o_ref, w1s_ref, w2s_ref):
    i = pl.program_id(0)

    @pl.when(i < nused_ref[0])
    def _():
        @pl.when(first_ref[i] == 1)
        def _():
            w1s_ref[...] = w1_ref[0, 0].astype(BF16)
            w2s_ref[...] = w2_ref[0, 0].astype(BF16)

        f = w2_ref.shape[2]
        gu = jnp.dot(xs_ref[...].astype(BF16), w1s_ref[...], preferred_element_type=F32) + b1_ref[0, 0]
        glu = jnp.minimum(gu[:, :f], SWIGLU_LIMIT)
        lin = jnp.clip(gu[:, f:], -SWIGLU_LIMIT, SWIGLU_LIMIT)
        act = glu * jax.nn.sigmoid(SWIGLU_ALPHA * glu) * (lin + 1.0)
        o_ref[...] = jnp.dot(act.astype(BF16), w2s_ref[...], preferred_element_type=F32) + b2_ref[0, 0]

    @pl.when(i >= nused_ref[0])
    def _():
        o_ref[...] = jnp.zeros_like(o_ref)


def _experts(l, xs, w1, b1, w2, b2, block_e, first, nused):
    P, D = xs.shape
    F2 = w1.shape[-1]
    F = w2.shape[2]
    nb = P // MOE_BLOCK
    row_blk = lambda i, be, fi, nu: (jnp.minimum(i, nu[0] - 1), 0)
    by_e = lambda i, be, fi, nu: (l, be[i], 0, 0)
    grid_spec = pltpu.PrefetchScalarGridSpec(
        num_scalar_prefetch=3,
        grid=(nb,),
        in_specs=[
            pl.BlockSpec((MOE_BLOCK, D), row_blk),
            pl.BlockSpec((1, 1, D, F2), by_e),
            pl.BlockSpec((1, 1, 1, F2), by_e),
            pl.BlockSpec((1, 1, F, D), by_e),
            pl.BlockSpec((1, 1, 1, D), by_e),
        ],
        out_specs=pl.BlockSpec((MOE_BLOCK, D), lambda i, be, fi, nu: (i, 0)),
        scratch_shapes=[pltpu.VMEM((D, F2), BF16), pltpu.VMEM((F, D), BF16)],
    )
    return pl.pallas_call(
        _experts_kernel,
        grid_spec=grid_spec,
        out_shape=jax.ShapeDtypeStruct((P, D), F32),
        compiler_params=_cparams(("arbitrary",)),
        name="experts",
    )(block_e, first, nused, xs, w1, b1, w2, b2)


def _combine_kernel(dest_ref, ob_ref, route_ref, x_ref, g2_ref, o_ref, buf_ref, sem):
    rows = x_ref.shape[0]

    def issue(r, carry):
        for kk in range(TOP_K):
            slot = dest_ref[0, 0, r * TOP_K + kk]
            pltpu.make_async_copy(ob_ref.at[pl.ds(slot, 1)], buf_ref.at[kk, pl.ds(r, 1)], sem).start()
        return carry

    lax.fori_loop(0, rows, issue, 0)
    for kk in range(TOP_K):
        pltpu.make_async_copy(ob_ref.at[pl.ds(0, rows)], buf_ref.at[kk], sem).wait()

    route = route_ref[...]
    y = jnp.zeros(x_ref.shape, F32)
    for kk in range(TOP_K):
        y = y + route[:, TOP_K + kk:TOP_K + kk + 1] * buf_ref[kk]
    o_ref[...] = x_ref[...] + g2_ref[0] * y


def _combine(out_buf, dest, route, x1, g2, seq):
    N, D = x1.shape
    rows = min(TOK_ROWS, N)
    steps = N // rows
    per_seq = seq // rows
    dest3 = dest.reshape(steps, 1, rows * TOP_K)
    return pl.pallas_call(
        _combine_kernel,
        grid=(steps,),
        in_specs=[
            pl.BlockSpec((1, 1, rows * TOP_K), lambda i: (i, 0, 0), memory_space=pltpu.SMEM),
            pl.BlockSpec(memory_space=pl.ANY),
            pl.BlockSpec((rows, LANES), lambda i: (i, 0)),
            pl.BlockSpec((rows, D), lambda i: (i, 0)),
            pl.BlockSpec((1, 1, D), lambda i: (i // per_seq, 0, 0)),
        ],
        out_specs=pl.BlockSpec((rows, D), lambda i: (i, 0)),
        out_shape=jax.ShapeDtypeStruct((N, D), F32),
        scratch_shapes=[pltpu.VMEM((TOP_K, rows, D), F32), pltpu.SemaphoreType.DMA(())],
        compiler_params=_cparams(("arbitrary",)),
        name="combine",
    )(dest3, out_buf, route, x1, g2)


def _slots(route, counts, n_tokens):
    idx = route[:, :TOP_K].astype(jnp.int32)
    rank = route[:, 2 * TOP_K:3 * TOP_K].astype(jnp.int32)
    counts = counts.astype(jnp.int32)
    padded = (counts + MOE_BLOCK - 1) // MOE_BLOCK * MOE_BLOCK
    pend = jnp.cumsum(padded)
    pstart = pend - padded
    experts = jnp.arange(N_EXPERTS, dtype=jnp.int32)
    dest = rank + jnp.sum(jnp.where(idx[..., None] == experts, pstart, 0), axis=-1)
    n_slots = n_tokens * TOP_K + N_EXPERTS * MOE_BLOCK
    n_blocks = n_slots // MOE_BLOCK
    nused = (pend[-1] // MOE_BLOCK).astype(jnp.int32)
    blk = jnp.minimum(jnp.arange(n_blocks, dtype=jnp.int32), nused - 1)
    block_e = jnp.sum((blk[:, None] * MOE_BLOCK >= pend[None, :]).astype(jnp.int32), axis=-1)
    first = jnp.concatenate([jnp.ones((1,), jnp.int32), (block_e[1:] != block_e[:-1]).astype(jnp.int32)])
    return dest.astype(jnp.int32), block_e, first, nused.reshape(1), n_slots


def kernel(x, c, norm1_g, w_mod, b_mod, w_in, q_norm_g, k_norm_g, rel_bias, pool_w, pool_b, pool_scale, w_out,
           norm2_g, router_w, router_b, exp_w1, exp_b1, exp_w2, exp_b2):
    B, S, D = x.shape
    L = w_mod.shape[0]
    N = B * S
    E = router_w.shape[-1]
    row3 = lambda a: a.reshape(L, 1, -1)
    mod = _modulation(c, w_mod, b_mod)
    w_in_b, w_out_b, pool_w_b, router_w_b = (a.astype(BF16) for a in (w_in, w_out, pool_w, router_w))
    qg = row3(jnp.tile(q_norm_g, (1, N_HEADS)))
    kg = row3(jnp.tile(k_norm_g, (1, N_HEADS)))
    n1, n2, pb, ps, rb = row3(norm1_g), row3(norm2_g), row3(pool_b), row3(pool_scale), row3(router_b)
    b1 = exp_b1.reshape(L, E, 1, -1)
    b2 = exp_b2.reshape(L, E, 1, -1)
    for l in range(L):
        sh1, sc1, g1, sh2, sc2, g2 = [mod[l, :, i * D:(i + 1) * D].reshape(B, 1, D) for i in range(6)]
        q, k, v, d = _mixer_in(l, x, sh1, sc1, n1, w_in_b, qg, kg)
        attn = _attention(q, k, v, _attn_bias(rel_bias[l]))
        x1, h2, route, counts = _mixer_out(l, x, attn, d, pool_w_b, pb, ps, w_out_b, g1, n2, sh2, sc2, router_w_b, rb)
        route = route.reshape(N, LANES)
        dest, block_e, first, nused, n_slots = _slots(route, counts[0], N)
        xs = _dispatch(h2.reshape(N, D), dest, n_slots)
        out_buf = _experts(l, xs, exp_w1, b1, exp_w2, b2, block_e, first, nused)
        x = _combine(out_buf, dest, route, x1.reshape(N, D), g2, S).reshape(B, S, D)
    return x
```

```python
import functools

import jax
import jax.numpy as jnp
from jax import lax
from jax.experimental import pallas as pl
from jax.experimental.pallas import tpu as pltpu

F32 = jnp.float32
BF16 = jnp.bfloat16

CHUNK = 64
LEFT_CHUNKS = 8
BAND = LEFT_CHUNKS + 1
HEAD_DIM = 64
N_HEADS = 8
ATTN_WIDTH = N_HEADS * HEAD_DIM
MAX_REL = 2 * CHUNK
POOL_WINDOWS = (2, 4, 8, 16)
POOL_GROUP = 128
POOL_WIDTH = POOL_GROUP * len(POOL_WINDOWS)
N_EXPERTS = 32
TOP_K = 4
SWIGLU_ALPHA = 1.702
SWIGLU_LIMIT = 7.0
MOE_BLOCK = 256
NORM_EPS = 1e-6
MASK_VALUE = -1e30

LANES = 128
POOL_HALO = 16

IN_ROWS = 512
ATTN_CHUNKS = 4
OUT_ROWS = 256
TOK_ROWS = 256
VMEM_LIMIT = 56 * 1024 * 1024


def _cparams(sem, vmem=VMEM_LIMIT):
    return pltpu.CompilerParams(dimension_semantics=sem, vmem_limit_bytes=vmem)


def _layer_spec(a, l):
    zeros = (0,) * (a.ndim - 1)
    return pl.BlockSpec((1,) + a.shape[1:], lambda *_: (l,) + zeros)


def _mod_kernel(c_ref, w_ref, b_ref, o_ref):
    c = c_ref[...]
    cond = c * jax.nn.sigmoid(c)
    o_ref[0] = jnp.dot(cond.astype(BF16), w_ref[0].astype(BF16), preferred_element_type=F32) + b_ref[0]


def _modulation(c, w_mod, b_mod):
    L, D, W = w_mod.shape
    B = c.shape[0]
    tn = 1024
    return pl.pallas_call(
        _mod_kernel,
        grid=(L, W // tn),
        in_specs=[
            pl.BlockSpec((B, D), lambda l, j: (0, 0)),
            pl.BlockSpec((1, D, tn), lambda l, j: (l, 0, j)),
            pl.BlockSpec((1, 1, tn), lambda l, j: (l, 0, j)),
        ],
        out_specs=pl.BlockSpec((1, B, tn), lambda l, j: (l, 0, j)),
        out_shape=jax.ShapeDtypeStruct((L, B, W), F32),
        compiler_params=_cparams(("arbitrary", "arbitrary")),
        name="modulation",
    )(c, w_mod, b_mod.reshape(L, 1, W))


def _head_norm(t, gain):
    rows = t.shape[0]
    lane = lax.broadcasted_iota(jnp.int32, (rows, LANES), 1)
    first = lane < HEAD_DIM
    outs = []
    for p in range(ATTN_WIDTH // LANES):
        blk = t[:, p * LANES:(p + 1) * LANES]
        sq = blk * blk
        sa = jnp.sum(jnp.where(first, sq, 0.0), axis=-1, keepdims=True)
        sb = jnp.sum(jnp.where(first, 0.0, sq), axis=-1, keepdims=True)
        ra = lax.rsqrt(sa * (1.0 / HEAD_DIM) + NORM_EPS)
        rb = lax.rsqrt(sb * (1.0 / HEAD_DIM) + NORM_EPS)
        outs.append(blk * jnp.where(first, ra, rb))
    return jnp.concatenate(outs, axis=-1) * gain


def _mixer_in_kernel(x_ref, sh_ref, sc_ref, g_ref, w_ref, qg_ref, kg_ref,
                     q_ref, k_ref, v_ref, d_ref, halo_ref):
    j = pl.program_id(1)
    rows = x_ref.shape[1]
    x = x_ref[0]
    ms = jnp.mean(x * x, axis=-1, keepdims=True)
    h = (x * lax.rsqrt(ms + NORM_EPS) * g_ref[0]) * (1.0 + sc_ref[0]) + sh_ref[0]
    z = jnp.dot(h.astype(BF16), w_ref[0], preferred_element_type=F32)
    aw = ATTN_WIDTH
    q_ref[0] = (_head_norm(z[:, :aw], qg_ref[0]) * (HEAD_DIM ** -0.5)).astype(q_ref.dtype)
    k_ref[0] = _head_norm(z[:, aw:2 * aw], kg_ref[0]).astype(k_ref.dtype)
    v_ref[0] = z[:, 2 * aw:3 * aw].astype(v_ref.dtype)

    u = z[:, 3 * aw:]

    @pl.when(j == 0)
    def _():
        halo_ref[...] = jnp.zeros_like(halo_ref)

    t_idx = j * rows + lax.broadcasted_iota(jnp.int32, (rows, 1), 0)
    for gi, w in enumerate(POOL_WINDOWS):
        ls = slice(gi * POOL_GROUP, (gi + 1) * POOL_GROUP)
        ug = u[:, ls]
        ext = jnp.concatenate([halo_ref[:, ls], ug], axis=0)
        acc = ext
        span = 1
        while span < w:
            acc = acc + pltpu.roll(acc, span, 0)
            span *= 2
        win = acc[POOL_HALO:, :]
        count = jnp.minimum(t_idx + 1, w).astype(F32)
        d_ref[0, :, ls] = (win / count - ug).astype(d_ref.dtype)
    halo_ref[...] = u[rows - POOL_HALO:, :]


def _mixer_in(l, x, sh, sc, g, w_in, qg, kg):
    B, S, D = x.shape
    rows = min(IN_ROWS, S)
    o_spec = pl.BlockSpec((1, rows, ATTN_WIDTH), lambda b, j: (b, j, 0))
    o_shape = jax.ShapeDtypeStruct((B, S, ATTN_WIDTH), BF16)
    per_b = pl.BlockSpec((1, 1, D), lambda b, j: (b, 0, 0))
    return pl.pallas_call(
        _mixer_in_kernel,
        grid=(B, S // rows),
        in_specs=[
            pl.BlockSpec((1, rows, D), lambda b, j: (b, j, 0)),
            per_b, per_b, _layer_spec(g, l), _layer_spec(w_in, l), _layer_spec(qg, l), _layer_spec(kg, l),
        ],
        out_specs=[o_spec, o_spec, o_spec, o_spec],
        out_shape=[o_shape, o_shape, o_shape, o_shape],
        scratch_shapes=[pltpu.VMEM((POOL_HALO, POOL_WIDTH), F32)],
        compiler_params=_cparams(("arbitrary", "arbitrary")),
        name="mixer_in",
    )(x, sh, sc, g, w_in, qg, kg)


def _attn_kernel(q_ref, k0_ref, k1_ref, k2_ref, v0_ref, v1_ref, v2_ref, diag_ref, o_ref, bias_ref):
    i = pl.program_id(1)
    tq = q_ref.shape[1]
    k_refs = (k0_ref, k1_ref, k2_ref)
    v_refs = (v0_ref, v1_ref, v2_ref)
    nt = len(k_refs)

    @pl.when((pl.program_id(0) == 0) & (i == 0))
    def _():
        row = lax.broadcasted_iota(jnp.int32, (tq, nt * tq), 0)
        col = lax.broadcasted_iota(jnp.int32, (tq, nt * tq), 1)
        in_band = col - (row // CHUNK) * CHUNK
        ok = (in_band >= 0) & (in_band < BAND * CHUNK)
        width = diag_ref.shape[1]
        for h in range(N_HEADS):
            table = jnp.broadcast_to(diag_ref[h:h + 1, :], (tq, width))
            toeplitz = pltpu.roll(table, width - tq, 1, stride=1, stride_axis=0)[:, :nt * tq]
            bias_ref[h // 2, (h % 2) * tq:(h % 2 + 1) * tq, :] = jnp.where(ok, toeplitz, MASK_VALUE)

    lane = lax.broadcasted_iota(jnp.int32, (tq, LANES), 1)
    first = lane < HEAD_DIM
    kpos = lax.broadcasted_iota(jnp.int32, (2 * tq, nt * tq), 1)
    in_seq = kpos >= (nt - 1 - i) * tq
    nt_dims = (((1,), (1,)), ((), ()))
    for p in range(ATTN_WIDTH // LANES):
        ls = slice(p * LANES, (p + 1) * LANES)
        q = q_ref[0, :, ls]
        zero = jnp.zeros_like(q)
        q2 = jnp.concatenate([jnp.where(first, q, zero), jnp.where(first, zero, q)], axis=0)
        s = jnp.concatenate(
            [lax.dot_general(q2, kr[0, :, ls], nt_dims, preferred_element_type=F32) for kr in k_refs], axis=1)
        s = jnp.where(in_seq, s + bias_ref[p], MASK_VALUE)
        m = jnp.max(s, axis=-1, keepdims=True)
        e = jnp.exp(s - m)
        l = jnp.sum(e, axis=-1, keepdims=True)
        eb = e.astype(BF16)
        pv = jnp.dot(eb[:, :tq], v_refs[0][0, :, ls], preferred_element_type=F32)
        for n in range(1, nt):
            pv = pv + jnp.dot(eb[:, n * tq:(n + 1) * tq], v_refs[n][0, :, ls], preferred_element_type=F32)
        pv = pv / l
        o_ref[0, :, ls] = jnp.where(first, pv[:tq], pv[tq:]).astype(o_ref.dtype)


def _attention(q, k, v, diag):
    B, S, W = q.shape
    tq = ATTN_CHUNKS * CHUNK
    assert LEFT_CHUNKS == 2 * ATTN_CHUNKS and S % tq == 0
    cur = pl.BlockSpec((1, tq, W), lambda b, i: (b, i, 0))
    back = lambda n: pl.BlockSpec((1, tq, W), lambda b, i: (b, jnp.maximum(i - n, 0), 0))
    return pl.pallas_call(
        _attn_kernel,
        grid=(B, S // tq),
        in_specs=[cur, back(2), back(1), cur, back(2), back(1), cur,
                  pl.BlockSpec(diag.shape, lambda b, i: (0, 0))],
        out_specs=cur,
        out_shape=jax.ShapeDtypeStruct((B, S, W), BF16),
        scratch_shapes=[pltpu.VMEM((N_HEADS // 2, 2 * tq, 3 * tq), F32)],
        compiler_params=_cparams(("arbitrary", "arbitrary")),
        name="attention",
    )(q, k, k, k, v, v, v, diag)


def _bias_diagonals(rel_bias):
    tq = ATTN_CHUNKS * CHUNK
    H = rel_bias.shape[0]
    lo = tq + LEFT_CHUNKS * CHUNK - MAX_REL
    hi = 4 * tq - lo - (2 * MAX_REL + 1)
    return jnp.concatenate([jnp.broadcast_to(rel_bias[:, :1], (H, lo)), rel_bias,
                            jnp.broadcast_to(rel_bias[:, -1:], (H, hi))], axis=1).astype(F32)


def _mixer_out_kernel(x_ref, a_ref, d_ref, pw_ref, pb_ref, ps_ref, wo_ref, g1_ref, n2_ref, sh_ref, sc_ref,
                      rw_ref, rb_ref, x1_ref, h2_ref, route_ref, cnt_ref, run_ref):
    step = pl.program_id(0) * pl.num_programs(1) + pl.program_id(1)
    rows = x_ref.shape[1]

    @pl.when(step == 0)
    def _():
        run_ref[...] = jnp.zeros_like(run_ref)

    y = jnp.dot(a_ref[0], wo_ref[0, :ATTN_WIDTH, :], preferred_element_type=F32)
    for gi in range(len(POOL_WINDOWS)):
        ls = slice(gi * POOL_GROUP, (gi + 1) * POOL_GROUP)
        pg = jnp.dot(d_ref[0, :, ls], pw_ref[0, gi], preferred_element_type=F32)
        pg = (pg + pb_ref[0, :, ls]) * ps_ref[0, :, ls]
        lo = ATTN_WIDTH + gi * POOL_GROUP
        y = y + jnp.dot(pg.astype(BF16), wo_ref[0, lo:lo + POOL_GROUP, :], preferred_element_type=F32)
    x1 = x_ref[0] + g1_ref[0] * y
    x1_ref[0] = x1

    ms = jnp.mean(x1 * x1, axis=-1, keepdims=True)
    h2 = (x1 * lax.rsqrt(ms + NORM_EPS) * n2_ref[0]) * (1.0 + sc_ref[0]) + sh_ref[0]
    h2_ref[0] = h2

    logits = jnp.dot(h2.astype(BF16), rw_ref[0], preferred_element_type=F32) + rb_ref[0]
    ne = logits.shape[-1]
    elane = lax.broadcasted_iota(jnp.int32, (rows, ne), 1)
    work = logits
    picked = jnp.zeros((rows, ne), F32)
    vals, idxs = [], []
    for _ in range(TOP_K):
        m = jnp.max(work, axis=-1, keepdims=True)
        idx = jnp.min(jnp.where(work == m, elane, ne), axis=-1, keepdims=True)
        sel = elane == idx
        vals.append(m)
        idxs.append(idx)
        work = jnp.where(sel, -jnp.inf, work)
        picked = picked + sel.astype(F32)
    exps = [jnp.exp(v - vals[0]) for v in vals]
    denom = exps[0] + exps[1] + exps[2] + exps[3]

    r_i = lax.broadcasted_iota(jnp.int32, (rows, rows), 0)
    c_i = lax.broadcasted_iota(jnp.int32, (rows, rows), 1)
    tri = (c_i < r_i).astype(BF16)
    before = jnp.dot(tri, picked.astype(BF16), preferred_element_type=F32) + run_ref[...]
    run_ref[...] = run_ref[...] + jnp.sum(picked, axis=0, keepdims=True)

    lane = lax.broadcasted_iota(jnp.int32, (rows, LANES), 1)
    route = jnp.zeros((rows, LANES), F32)
    for kk in range(TOP_K):
        rank = jnp.sum(jnp.where(elane == idxs[kk], before, 0.0), axis=-1, keepdims=True)
        route = jnp.where(lane == kk, idxs[kk].astype(F32), route)
        route = jnp.where(lane == TOP_K + kk, exps[kk] / denom, route)
        route = jnp.where(lane == 2 * TOP_K + kk, rank, route)
    route_ref[0] = route
    cnt_ref[...] = jnp.broadcast_to(run_ref[...], cnt_ref.shape)


def _mixer_out(l, x, attn, d, pool_w, pool_b, pool_scale, w_out, g1, n2, sh2, sc2, router_w, router_b):
    B, S, D = x.shape
    rows = min(OUT_ROWS, S)
    E = router_w.shape[-1]
    tile = lambda w: pl.BlockSpec((1, rows, w), lambda b, j: (b, j, 0))
    per_b = pl.BlockSpec((1, 1, D), lambda b, j: (b, 0, 0))
    lay = lambda a: _layer_spec(a, l)
    return pl.pallas_call(
        _mixer_out_kernel,
        grid=(B, S // rows),
        in_specs=[
            tile(D), tile(ATTN_WIDTH), tile(POOL_WIDTH),
            lay(pool_w), lay(pool_b), lay(pool_scale), lay(w_out),
            per_b, lay(n2), per_b, per_b, lay(router_w), lay(router_b),
        ],
        out_specs=[tile(D), tile(D), tile(LANES), pl.BlockSpec((8, E), lambda b, j: (0, 0))],
        out_shape=[
            jax.ShapeDtypeStruct((B, S, D), F32),
            jax.ShapeDtypeStruct((B, S, D), F32),
            jax.ShapeDtypeStruct((B, S, LANES), F32),
            jax.ShapeDtypeStruct((8, E), F32),
        ],
        scratch_shapes=[pltpu.VMEM((1, E), F32)],
        compiler_params=_cparams(("arbitrary", "arbitrary")),
        name="mixer_out",
    )(x, attn, d, pool_w, pool_b, pool_scale, w_out, g1, n2, sh2, sc2, router_w, router_b)


def _dispatch_kernel(dest_ref, h_ref, xs_in_ref, xs_ref, sem):
    del xs_in_ref
    rows = h_ref.shape[0]

    def row_copy(r, slot):
        return pltpu.make_async_copy(h_ref.at[pl.ds(r, 1)], xs_ref.at[pl.ds(slot, 1)], sem)

    def issue(r, carry):
        for kk in range(TOP_K):
            row_copy(r, dest_ref[0, 0, r * TOP_K + kk]).start()
        return carry

    lax.fori_loop(0, rows, issue, 0, unroll=8)
    for _ in range(TOP_K):
        pltpu.make_async_copy(h_ref, xs_ref.at[pl.ds(0, rows)], sem).wait()


def _dispatch(h2, dest, n_slots):
    N, D = h2.shape
    rows = min(TOK_ROWS, N)
    steps = N // rows
    dest3 = dest.reshape(steps, 1, rows * TOP_K)
    return pl.pallas_call(
        _dispatch_kernel,
        grid=(steps,),
        in_specs=[
            pl.BlockSpec((1, 1, rows * TOP_K), lambda i: (i, 0, 0), memory_space=pltpu.SMEM),
            pl.BlockSpec((rows, D), lambda i: (i, 0)),
            pl.BlockSpec(memory_space=pl.ANY),
        ],
        out_specs=pl.BlockSpec(memory_space=pl.ANY),
        out_shape=jax.ShapeDtypeStruct((n_slots, D), h2.dtype),
        scratch_shapes=[pltpu.SemaphoreType.DMA(())],
        input_output_aliases={2: 0},
        compiler_params=_cparams(("arbitrary",)),
        name="dispatch",
    )(dest3, h2, jnp.zeros((n_slots, D), h2.dtype))


def _experts_kernel(be_ref, first_ref, nused_ref, xs_ref, w1_ref, b1_ref, w2_ref, b2_ref, o_ref, w1s_ref, w2s_ref):
    i = pl.program_id(0)

    @pl.when(i < nused_ref[0])
    def _():
        @pl.when(first_ref[i] == 1)
        def _():
            w1s_ref[...] = w1_ref[0, 0].astype(BF16)
            w2s_ref[...] = w2_ref[0, 0].astype(BF16)

        f = w2_ref.shape[2]
        gu = jnp.dot(xs_ref[...].astype(BF16), w1s_ref[...], preferred_element_type=F32) + b1_ref[0, 0]
        glu = jnp.minimum(gu[:, :f], SWIGLU_LIMIT)
        lin = jnp.clip(gu[:, f:], -SWIGLU_LIMIT, SWIGLU_LIMIT)
        act = glu * jax.nn.sigmoid(SWIGLU_ALPHA * glu) * (lin + 1.0)
        o_ref[...] = jnp.dot(act.astype(BF16), w2s_ref[...], preferred_element_type=F32) + b2_ref[0, 0]

    @pl.when(i >= nused_ref[0])
    def _():
        o_ref[...] = jnp.zeros_like(o_ref)


def _experts(l, xs, w1, b1, w2, b2, block_e, first, nused):
    P, D = xs.shape
    F2 = w1.shape[-1]
    F = w2.shape[2]
    nb = P // MOE_BLOCK
    row_blk = lambda i, be, fi, nu: (jnp.minimum(i, nu[0] - 1), 0)
    by_e = lambda i, be, fi, nu: (l, be[i], 0, 0)
    grid_spec = pltpu.PrefetchScalarGridSpec(
        num_scalar_prefetch=3,
        grid=(nb,),
        in_specs=[
            pl.BlockSpec((MOE_BLOCK, D), row_blk),
            pl.BlockSpec((1, 1, D, F2), by_e),
            pl.BlockSpec((1, 1, 1, F2), by_e),
            pl.BlockSpec((1, 1, F, D), by_e),
            pl.BlockSpec((1, 1, 1, D), by_e),
        ],
        out_specs=pl.BlockSpec((MOE_BLOCK, D), lambda i, be, fi, nu: (i, 0)),
        scratch_shapes=[pltpu.VMEM((D, F2), BF16), pltpu.VMEM((F, D), BF16)],
    )
    return pl.pallas_call(
        _experts_kernel,
        grid_spec=grid_spec,
        out_shape=jax.ShapeDtypeStruct((P, D), F32),
        compiler_params=_cparams(("arbitrary",)),
        name="experts",
    )(block_e, first, nused, xs, w1, b1, w2, b2)


def _combine_kernel(dest_ref, ob_ref, route_ref, x_ref, g2_ref, o_ref, buf_ref, sem):
    rows = x_ref.shape[0]

    def issue(r, carry):
        for kk in range(TOP_K):
            slot = dest_ref[0, 0, r * TOP_K + kk]
            pltpu.make_async_copy(ob_ref.at[pl.ds(slot, 1)], buf_ref.at[kk, pl.ds(r, 1)], sem).start()
        return carry

    lax.fori_loop(0, rows, issue, 0, unroll=8)
    for kk in range(TOP_K):
        pltpu.make_async_copy(ob_ref.at[pl.ds(0, rows)], buf_ref.at[kk], sem).wait()

    route = route_ref[...]
    y = jnp.zeros(x_ref.shape, F32)
    for kk in range(TOP_K):
        y = y + route[:, TOP_K + kk:TOP_K + kk + 1] * buf_ref[kk]
    o_ref[...] = x_ref[...] + g2_ref[0] * y


def _combine(out_buf, dest, route, x1, g2, seq):
    N, D = x1.shape
    rows = min(TOK_ROWS, N)
    steps = N // rows
    per_seq = seq // rows
    dest3 = dest.reshape(steps, 1, rows * TOP_K)
    return pl.pallas_call(
        _combine_kernel,
        grid=(steps,),
        in_specs=[
            pl.BlockSpec((1, 1, rows * TOP_K), lambda i: (i, 0, 0), memory_space=pltpu.SMEM),
            pl.BlockSpec(memory_space=pl.ANY),
            pl.BlockSpec((rows, LANES), lambda i: (i, 0)),
            pl.BlockSpec((rows, D), lambda i: (i, 0)),
            pl.BlockSpec((1, 1, D), lambda i: (i // per_seq, 0, 0)),
        ],
        out_specs=pl.BlockSpec((rows, D), lambda i: (i, 0)),
        out_shape=jax.ShapeDtypeStruct((N, D), F32),
        scratch_shapes=[pltpu.VMEM((TOP_K, rows, D), F32), pltpu.SemaphoreType.DMA(())],
        compiler_params=_cparams(("arbitrary",)),
        name="combine",
    )(dest3, out_buf, route, x1, g2)


def _slots(route, counts, n_tokens):
    idx = route[:, :TOP_K].astype(jnp.int32)
    rank = route[:, 2 * TOP_K:3 * TOP_K].astype(jnp.int32)
    counts = counts.astype(jnp.int32)
    padded = (counts + MOE_BLOCK - 1) // MOE_BLOCK * MOE_BLOCK
    pend = jnp.cumsum(padded)
    pstart = pend - padded
    experts = jnp.arange(N_EXPERTS, dtype=jnp.int32)
    dest = rank + jnp.sum(jnp.where(idx[..., None] == experts, pstart, 0), axis=-1)
    n_slots = n_tokens * TOP_K + N_EXPERTS * MOE_BLOCK
    n_blocks = n_slots // MOE_BLOCK
    nused = (pend[-1] // MOE_BLOCK).astype(jnp.int32)
    blk = jnp.minimum(jnp.arange(n_blocks, dtype=jnp.int32), nused - 1)
    block_e = jnp.sum((blk[:, None] * MOE_BLOCK >= pend[None, :]).astype(jnp.int32), axis=-1)
    first = jnp.concatenate([jnp.ones((1,), jnp.int32), (block_e[1:] != block_e[:-1]).astype(jnp.int32)])
    return dest.astype(jnp.int32), block_e, first, nused.reshape(1), n_slots


def kernel(x, c, norm1_g, w_mod, b_mod, w_in, q_norm_g, k_norm_g, rel_bias, pool_w, pool_b, pool_scale, w_out,
           norm2_g, router_w, router_b, exp_w1, exp_b1, exp_w2, exp_b2):
    B, S, D = x.shape
    L = w_mod.shape[0]
    N = B * S
    E = router_w.shape[-1]
    row3 = lambda a: a.reshape(L, 1, -1)
    mod = _modulation(c, w_mod, b_mod)
    w_in_b, w_out_b, pool_w_b, router_w_b = (a.astype(BF16) for a in (w_in, w_out, pool_w, router_w))
    qg = row3(jnp.tile(q_norm_g, (1, N_HEADS)))
    kg = row3(jnp.tile(k_norm_g, (1, N_HEADS)))
    n1, n2, pb, ps, rb = row3(norm1_g), row3(norm2_g), row3(pool_b), row3(pool_scale), row3(router_b)
    b1 = exp_b1.reshape(L, E, 1, -1)
    b2 = exp_b2.reshape(L, E, 1, -1)
    for l in range(L):
        sh1, sc1, g1, sh2, sc2, g2 = [mod[l, :, i * D:(i + 1) * D].reshape(B, 1, D) for i in range(6)]
        q, k, v, d = _mixer_in(l, x, sh1, sc1, n1, w_in_b, qg, kg)
        attn = _attention(q, k, v, _bias_diagonals(rel_bias[l]))
        x1, h2, route, counts = _mixer_out(l, x, attn, d, pool_w_b, pb, ps, w_out_b, g1, n2, sh2, sc2, router_w_b, rb)
        route = route.reshape(N, LANES)
        dest, block_e, first, nused, n_slots = _slots(route, counts[0], N)
        xs = _dispatch(h2.reshape(N, D), dest, n_slots)
        out_buf = _experts(l, xs, exp_w1, b1, exp_w2, b2, block_e, first, nused)
        x = _combine(out_buf, dest, route, x1.reshape(N, D), g2, S).reshape(B, S, D)
    return x
```

```python
import functools

import jax
import jax.numpy as jnp
from jax import lax
from jax.experimental import pallas as pl
from jax.experimental.pallas import tpu as pltpu

F32 = jnp.float32
BF16 = jnp.bfloat16

CHUNK = 64
LEFT_CHUNKS = 8
BAND = LEFT_CHUNKS + 1
HEAD_DIM = 64
N_HEADS = 8
ATTN_WIDTH = N_HEADS * HEAD_DIM
MAX_REL = 2 * CHUNK
POOL_WINDOWS = (2, 4, 8, 16)
POOL_GROUP = 128
POOL_WIDTH = POOL_GROUP * len(POOL_WINDOWS)
N_EXPERTS = 32
TOP_K = 4
SWIGLU_ALPHA = 1.702
SWIGLU_LIMIT = 7.0
MOE_BLOCK = 256
NORM_EPS = 1e-6
MASK_VALUE = -1e30

LANES = 128
TOKEN_TILE = 8
POOL_HALO = 16

IN_ROWS = 512
ATTN_CHUNKS = 4
OUT_ROWS = 256
TOK_ROWS = 256
VMEM_LIMIT = 56 * 1024 * 1024


def _cparams(sem, vmem=VMEM_LIMIT):
    return pltpu.CompilerParams(dimension_semantics=sem, vmem_limit_bytes=vmem)


def _layer_spec(a, l):
    zeros = (0,) * (a.ndim - 1)
    return pl.BlockSpec((1,) + a.shape[1:], lambda *_: (l,) + zeros)


def _mod_kernel(c_ref, w_ref, b_ref, o_ref):
    c = c_ref[...]
    cond = c * jax.nn.sigmoid(c)
    o_ref[0] = jnp.dot(cond.astype(BF16), w_ref[0].astype(BF16), preferred_element_type=F32) + b_ref[0]


def _modulation(c, w_mod, b_mod):
    L, D, W = w_mod.shape
    B = c.shape[0]
    tn = 1024
    return pl.pallas_call(
        _mod_kernel,
        grid=(L, W // tn),
        in_specs=[
            pl.BlockSpec((B, D), lambda l, j: (0, 0)),
            pl.BlockSpec((1, D, tn), lambda l, j: (l, 0, j)),
            pl.BlockSpec((1, 1, tn), lambda l, j: (l, 0, j)),
        ],
        out_specs=pl.BlockSpec((1, B, tn), lambda l, j: (l, 0, j)),
        out_shape=jax.ShapeDtypeStruct((L, B, W), F32),
        compiler_params=_cparams(("arbitrary", "arbitrary")),
        name="modulation",
    )(c, w_mod, b_mod.reshape(L, 1, W))


def _head_norm(t, gain):
    rows = t.shape[0]
    lane = lax.broadcasted_iota(jnp.int32, (rows, LANES), 1)
    first = lane < HEAD_DIM
    outs = []
    for p in range(ATTN_WIDTH // LANES):
        blk = t[:, p * LANES:(p + 1) * LANES]
        sq = blk * blk
        sa = jnp.sum(jnp.where(first, sq, 0.0), axis=-1, keepdims=True)
        sb = jnp.sum(jnp.where(first, 0.0, sq), axis=-1, keepdims=True)
        ra = lax.rsqrt(sa * (1.0 / HEAD_DIM) + NORM_EPS)
        rb = lax.rsqrt(sb * (1.0 / HEAD_DIM) + NORM_EPS)
        outs.append(blk * jnp.where(first, ra, rb))
    return jnp.concatenate(outs, axis=-1) * gain


def _mixer_in_kernel(x_ref, sh_ref, sc_ref, g_ref, w_ref, qg_ref, kg_ref,
                     q_ref, k_ref, v_ref, d_ref, halo_ref):
    j = pl.program_id(1)
    rows = x_ref.shape[1]
    x = x_ref[0]
    ms = jnp.mean(x * x, axis=-1, keepdims=True)
    h = (x * lax.rsqrt(ms + NORM_EPS) * g_ref[0]) * (1.0 + sc_ref[0]) + sh_ref[0]
    z = jnp.dot(h.astype(BF16), w_ref[0], preferred_element_type=F32)
    aw = ATTN_WIDTH
    q_ref[0] = (_head_norm(z[:, :aw], qg_ref[0]) * (HEAD_DIM ** -0.5)).astype(q_ref.dtype)
    k_ref[0] = _head_norm(z[:, aw:2 * aw], kg_ref[0]).astype(k_ref.dtype)
    v_ref[0] = z[:, 2 * aw:3 * aw].astype(v_ref.dtype)

    u = z[:, 3 * aw:]

    @pl.when(j == 0)
    def _():
        halo_ref[...] = jnp.zeros_like(halo_ref)

    t_idx = j * rows + lax.broadcasted_iota(jnp.int32, (rows, 1), 0)
    for gi, w in enumerate(POOL_WINDOWS):
        ls = slice(gi * POOL_GROUP, (gi + 1) * POOL_GROUP)
        ug = u[:, ls]
        ext = jnp.concatenate([halo_ref[:, ls], ug], axis=0)
        acc = ext
        span = 1
        while span < w:
            acc = acc + pltpu.roll(acc, span, 0)
            span *= 2
        win = acc[POOL_HALO:, :]
        count = jnp.minimum(t_idx + 1, w).astype(F32)
        d_ref[0, :, ls] = (win / count - ug).astype(d_ref.dtype)
    halo_ref[...] = u[rows - POOL_HALO:, :]


def _mixer_in(l, x, sh, sc, g, w_in, qg, kg):
    B, S, D = x.shape
    rows = min(IN_ROWS, S)
    o_spec = pl.BlockSpec((1, rows, ATTN_WIDTH), lambda b, j: (b, j, 0))
    o_shape = jax.ShapeDtypeStruct((B, S, ATTN_WIDTH), BF16)
    per_b = pl.BlockSpec((1, 1, D), lambda b, j: (b, 0, 0))
    return pl.pallas_call(
        _mixer_in_kernel,
        grid=(B, S // rows),
        in_specs=[
            pl.BlockSpec((1, rows, D), lambda b, j: (b, j, 0)),
            per_b, per_b, _layer_spec(g, l), _layer_spec(w_in, l), _layer_spec(qg, l), _layer_spec(kg, l),
        ],
        out_specs=[o_spec, o_spec, o_spec, o_spec],
        out_shape=[o_shape, o_shape, o_shape, o_shape],
        scratch_shapes=[pltpu.VMEM((POOL_HALO, POOL_WIDTH), F32)],
        compiler_params=_cparams(("arbitrary", "arbitrary")),
        name="mixer_in",
    )(x, sh, sc, g, w_in, qg, kg)


def _attn_kernel(q_ref, k0_ref, k1_ref, k2_ref, v0_ref, v1_ref, v2_ref, diag_ref, o_ref, bias_ref):
    i = pl.program_id(1)
    tq = q_ref.shape[1]
    k_refs = (k0_ref, k1_ref, k2_ref)
    v_refs = (v0_ref, v1_ref, v2_ref)
    nt = len(k_refs)

    @pl.when((pl.program_id(0) == 0) & (i == 0))
    def _():
        row = lax.broadcasted_iota(jnp.int32, (tq, nt * tq), 0)
        col = lax.broadcasted_iota(jnp.int32, (tq, nt * tq), 1)
        in_band = col - (row // CHUNK) * CHUNK
        ok = (in_band >= 0) & (in_band < BAND * CHUNK)
        width = diag_ref.shape[1]
        for h in range(N_HEADS):
            table = jnp.broadcast_to(diag_ref[h:h + 1, :], (tq, width))
            toeplitz = pltpu.roll(table, width - tq, 1, stride=1, stride_axis=0)[:, :nt * tq]
            bias_ref[h // 2, (h % 2) * tq:(h % 2 + 1) * tq, :] = jnp.where(ok, toeplitz, MASK_VALUE)

    lane = lax.broadcasted_iota(jnp.int32, (tq, LANES), 1)
    first = lane < HEAD_DIM
    kpos = lax.broadcasted_iota(jnp.int32, (2 * tq, nt * tq), 1)
    in_seq = kpos >= (nt - 1 - i) * tq
    nt_dims = (((1,), (1,)), ((), ()))
    for p in range(ATTN_WIDTH // LANES):
        ls = slice(p * LANES, (p + 1) * LANES)
        q = q_ref[0, :, ls]
        zero = jnp.zeros_like(q)
        q2 = jnp.concatenate([jnp.where(first, q, zero), jnp.where(first, zero, q)], axis=0)
        s = jnp.concatenate(
            [lax.dot_general(q2, kr[0, :, ls], nt_dims, preferred_element_type=F32) for kr in k_refs], axis=1)
        s = jnp.where(in_seq, s + bias_ref[p], MASK_VALUE)
        m = jnp.max(s, axis=-1, keepdims=True)
        e = jnp.exp(s - m)
        l = jnp.sum(e, axis=-1, keepdims=True)
        eb = e.astype(BF16)
        pv = jnp.dot(eb[:, :tq], v_refs[0][0, :, ls], preferred_element_type=F32)
        for n in range(1, nt):
            pv = pv + jnp.dot(eb[:, n * tq:(n + 1) * tq], v_refs[n][0, :, ls], preferred_element_type=F32)
        pv = pv / l
        o_ref[0, :, ls] = jnp.where(first, pv[:tq], pv[tq:]).astype(o_ref.dtype)


def _attention(q, k, v, diag):
    B, S, W = q.shape
    tq = ATTN_CHUNKS * CHUNK
    assert LEFT_CHUNKS == 2 * ATTN_CHUNKS and S % tq == 0
    cur = pl.BlockSpec((1, tq, W), lambda b, i: (b, i, 0))
    back = lambda n: pl.BlockSpec((1, tq, W), lambda b, i: (b, jnp.maximum(i - n, 0), 0))
    return pl.pallas_call(
        _attn_kernel,
        grid=(B, S // tq),
        in_specs=[cur, back(2), back(1), cur, back(2), back(1), cur,
                  pl.BlockSpec(diag.shape, lambda b, i: (0, 0))],
        out_specs=cur,
        out_shape=jax.ShapeDtypeStruct((B, S, W), BF16),
        scratch_shapes=[pltpu.VMEM((N_HEADS // 2, 2 * tq, 3 * tq), F32)],
        compiler_params=_cparams(("arbitrary", "arbitrary")),
        name="attention",
    )(q, k, k, k, v, v, v, diag)


def _bias_diagonals(rel_bias):
    tq = ATTN_CHUNKS * CHUNK
    H = rel_bias.shape[0]
    lo = tq + LEFT_CHUNKS * CHUNK - MAX_REL
    hi = 4 * tq - lo - (2 * MAX_REL + 1)
    return jnp.concatenate([jnp.broadcast_to(rel_bias[:, :1], (H, lo)), rel_bias,
                            jnp.broadcast_to(rel_bias[:, -1:], (H, hi))], axis=1).astype(F32)


def _mixer_out_kernel(x_ref, a_ref, d_ref, pw_ref, pb_ref, ps_ref, wo_ref, g1_ref, n2_ref, sh_ref, sc_ref,
                      rw_ref, rb_ref, x1_ref, h2_ref, route_ref, cnt_ref, run_ref):
    step = pl.program_id(0) * pl.num_programs(1) + pl.program_id(1)
    rows = x_ref.shape[1]

    @pl.when(step == 0)
    def _():
        run_ref[...] = jnp.zeros_like(run_ref)

    y = jnp.dot(a_ref[0], wo_ref[0, :ATTN_WIDTH, :], preferred_element_type=F32)
    for gi in range(len(POOL_WINDOWS)):
        ls = slice(gi * POOL_GROUP, (gi + 1) * POOL_GROUP)
        pg = jnp.dot(d_ref[0, :, ls], pw_ref[0, gi], preferred_element_type=F32)
        pg = (pg + pb_ref[0, :, ls]) * ps_ref[0, :, ls]
        lo = ATTN_WIDTH + gi * POOL_GROUP
        y = y + jnp.dot(pg.astype(BF16), wo_ref[0, lo:lo + POOL_GROUP, :], preferred_element_type=F32)
    x1 = x_ref[0] + g1_ref[0] * y
    x1_ref[0] = x1

    ms = jnp.mean(x1 * x1, axis=-1, keepdims=True)
    h2 = (x1 * lax.rsqrt(ms + NORM_EPS) * n2_ref[0]) * (1.0 + sc_ref[0]) + sh_ref[0]
    for s in range(TOKEN_TILE):
        h2_ref[pl.ds(s, rows, stride=TOKEN_TILE), :] = h2[:, s * LANES:(s + 1) * LANES]

    logits = jnp.dot(h2.astype(BF16), rw_ref[0], preferred_element_type=F32) + rb_ref[0]
    ne = logits.shape[-1]
    elane = lax.broadcasted_iota(jnp.int32, (rows, ne), 1)
    work = logits
    picked = jnp.zeros((rows, ne), F32)
    vals, idxs = [], []
    for _ in range(TOP_K):
        m = jnp.max(work, axis=-1, keepdims=True)
        idx = jnp.min(jnp.where(work == m, elane, ne), axis=-1, keepdims=True)
        sel = elane == idx
        vals.append(m)
        idxs.append(idx)
        work = jnp.where(sel, -jnp.inf, work)
        picked = picked + sel.astype(F32)
    exps = [jnp.exp(v - vals[0]) for v in vals]
    denom = exps[0] + exps[1] + exps[2] + exps[3]

    r_i = lax.broadcasted_iota(jnp.int32, (rows, rows), 0)
    c_i = lax.broadcasted_iota(jnp.int32, (rows, rows), 1)
    tri = (c_i < r_i).astype(BF16)
    before = jnp.dot(tri, picked.astype(BF16), preferred_element_type=F32) + run_ref[...]
    run_ref[...] = run_ref[...] + jnp.sum(picked, axis=0, keepdims=True)

    lane = lax.broadcasted_iota(jnp.int32, (rows, LANES), 1)
    route = jnp.zeros((rows, LANES), F32)
    for kk in range(TOP_K):
        rank = jnp.sum(jnp.where(elane == idxs[kk], before, 0.0), axis=-1, keepdims=True)
        route = jnp.where(lane == kk, idxs[kk].astype(F32), route)
        route = jnp.where(lane == TOP_K + kk, exps[kk] / denom, route)
        route = jnp.where(lane == 2 * TOP_K + kk, rank, route)
    route_ref[0] = route
    cnt_ref[...] = jnp.broadcast_to(run_ref[...], cnt_ref.shape)


def _mixer_out(l, x, attn, d, pool_w, pool_b, pool_scale, w_out, g1, n2, sh2, sc2, router_w, router_b):
    B, S, D = x.shape
    rows = min(OUT_ROWS, S)
    E = router_w.shape[-1]
    tile = lambda w: pl.BlockSpec((1, rows, w), lambda b, j: (b, j, 0))
    per_b = pl.BlockSpec((1, 1, D), lambda b, j: (b, 0, 0))
    lay = lambda a: _layer_spec(a, l)
    return pl.pallas_call(
        _mixer_out_kernel,
        grid=(B, S // rows),
        in_specs=[
            tile(D), tile(ATTN_WIDTH), tile(POOL_WIDTH),
            lay(pool_w), lay(pool_b), lay(pool_scale), lay(w_out),
            per_b, lay(n2), per_b, per_b, lay(router_w), lay(router_b),
        ],
        out_specs=[tile(D), pl.BlockSpec((rows * TOKEN_TILE, LANES), lambda b, j: (b * (S // rows) + j, 0)),
                   tile(LANES), pl.BlockSpec((8, E), lambda b, j: (0, 0))],
        out_shape=[
            jax.ShapeDtypeStruct((B, S, D), F32),
            jax.ShapeDtypeStruct((B * S * TOKEN_TILE, LANES), F32),
            jax.ShapeDtypeStruct((B, S, LANES), F32),
            jax.ShapeDtypeStruct((8, E), F32),
        ],
        scratch_shapes=[pltpu.VMEM((1, E), F32)],
        compiler_params=_cparams(("arbitrary", "arbitrary")),
        name="mixer_out",
    )(x, attn, d, pool_w, pool_b, pool_scale, w_out, g1, n2, sh2, sc2, router_w, router_b)


def _experts_kernel(be_ref, first_ref, nused_ref, tok_cur_ref, tok_next_ref, dst_prev_ref, dst_cur_ref,
                    h_ref, w1_ref, b1_ref, w2_ref, b2_ref, y_ref,
                    xa_ref, xb_ref, oa_ref, ob_ref, w1s_ref, w2s_ref, gsem, ssem, zsem, *, dump_start, n_dump):
    i = pl.program_id(0)
    nused = nused_ref[0]
    rows = MOE_BLOCK
    tr = TOKEN_TILE
    f = w2_ref.shape[2]

    def tile_in(tok, r, xbuf, sem):
        return pltpu.make_async_copy(h_ref.at[pl.ds(pl.multiple_of(tok * tr, tr), tr)], xbuf.at[pl.ds(r * tr, tr)], sem)

    def tile_out(obuf, r, dst, sem):
        return pltpu.make_async_copy(obuf.at[pl.ds(r * tr, tr)], y_ref.at[pl.ds(pl.multiple_of(dst * tr, tr), tr)], sem)

    def wait_in(xbuf, sem):
        pltpu.make_async_copy(h_ref.at[pl.ds(0, rows * tr)], xbuf, sem).wait()

    def wait_out(obuf, sem):
        pltpu.make_async_copy(obuf, y_ref.at[pl.ds(0, rows * tr)], sem).wait()

    def step(par):
        xcur, xnext = (xa_ref, xb_ref) if par == 0 else (xb_ref, xa_ref)
        ocur, oprev = (oa_ref, ob_ref) if par == 0 else (ob_ref, oa_ref)
        g_cur, g_next = gsem.at[par], gsem.at[1 - par]
        s_cur, s_prev = ssem.at[par], ssem.at[1 - par]

        if par == 0:
            @pl.when(i == 0)
            def _():
                oa_ref[...] = jnp.zeros_like(oa_ref)
                ob_ref[...] = jnp.zeros_like(ob_ref)
                for c in range(n_dump):
                    pltpu.make_async_copy(oa_ref, y_ref.at[pl.ds((dump_start + c * rows) * tr, rows * tr)], zsem).start()
                for c in range(n_dump):
                    pltpu.make_async_copy(oa_ref, y_ref.at[pl.ds((dump_start + c * rows) * tr, rows * tr)], zsem).wait()

                def first_gather(r, carry):
                    tile_in(tok_cur_ref[0, 0, r], r, xa_ref, gsem.at[0]).start()
                    return carry
                lax.fori_loop(0, rows, first_gather, 0, unroll=8)

        wait_in(xcur, g_cur)

        @pl.when(i >= 1)
        def _():
            wait_out(ocur, s_cur)

        @pl.when(first_ref[i] == 1)
        def _():
            w1s_ref[...] = w1_ref[0, 0].astype(BF16)
            w2s_ref[...] = w2_ref[0, 0].astype(BF16)

        for r in range(rows):
            tile_out(oprev, r, dst_prev_ref[0, 0, r], s_prev).start()
        for r in range(rows):
            tile_in(tok_next_ref[0, 0, r], r, xnext, g_next).start()
        x = jnp.concatenate([xcur[pl.ds(s, rows, stride=tr), :] for s in range(tr)], axis=1).astype(BF16)
        gu = jnp.dot(x, w1s_ref[...], preferred_element_type=F32) + b1_ref[0, 0]
        glu = jnp.minimum(gu[:, :f], SWIGLU_LIMIT)
        lin = jnp.clip(gu[:, f:], -SWIGLU_LIMIT, SWIGLU_LIMIT)
        act = glu * jax.nn.sigmoid(SWIGLU_ALPHA * glu) * (lin + 1.0)
        o = jnp.dot(act.astype(BF16), w2s_ref[...], preferred_element_type=F32) + b2_ref[0, 0]
        for s in range(tr):
            ocur[pl.ds(s, rows, stride=tr), :] = o[:, s * LANES:(s + 1) * LANES]

        @pl.when(i == nused - 1)
        def _():
            def last_scatter(r, carry):
                tile_out(ocur, r, dst_cur_ref[0, 0, r], s_cur).start()
                return carry
            lax.fori_loop(0, rows, last_scatter, 0, unroll=8)
            wait_out(oprev, s_prev)
            wait_out(ocur, s_cur)
            wait_in(xnext, g_next)

    @pl.when((i < nused) & (i % 2 == 0))
    def _():
        step(0)

    @pl.when((i < nused) & (i % 2 == 1))
    def _():
        step(1)


def _experts(l, h2t, w1, b1, w2, b2, block_e, first, nused, tok3, dst3, n_rows, dump_start, n_dump):
    D, F2 = w1.shape[2:]
    F = w2.shape[2]
    nb = tok3.shape[0]
    idx_blk = (1, 1, MOE_BLOCK)
    smem = lambda fn: pl.BlockSpec(idx_blk, fn, memory_space=pltpu.SMEM)
    by_e = lambda i, be, fi, nu: (l, be[i], 0, 0)
    block_rows = MOE_BLOCK * TOKEN_TILE
    grid_spec = pltpu.PrefetchScalarGridSpec(
        num_scalar_prefetch=3,
        grid=(nb,),
        in_specs=[
            smem(lambda i, be, fi, nu: (i, 0, 0)),
            smem(lambda i, be, fi, nu: (jnp.minimum(i + 1, nb - 1), 0, 0)),
            smem(lambda i, be, fi, nu: (jnp.where(i == 0, nb, i - 1), 0, 0)),
            smem(lambda i, be, fi, nu: (i, 0, 0)),
            pl.BlockSpec(memory_space=pl.ANY),
            pl.BlockSpec((1, 1, D, F2), by_e),
            pl.BlockSpec((1, 1, 1, F2), by_e),
            pl.BlockSpec((1, 1, F, D), by_e),
            pl.BlockSpec((1, 1, 1, D), by_e),
        ],
        out_specs=pl.BlockSpec(memory_space=pl.ANY),
        scratch_shapes=[pltpu.VMEM((block_rows, LANES), F32)] * 4 + [
            pltpu.VMEM((D, F2), BF16), pltpu.VMEM((F, D), BF16),
            pltpu.SemaphoreType.DMA((2,)), pltpu.SemaphoreType.DMA((2,)), pltpu.SemaphoreType.DMA(())],
    )
    return pl.pallas_call(
        functools.partial(_experts_kernel, dump_start=dump_start, n_dump=n_dump),
        grid_spec=grid_spec,
        out_shape=jax.ShapeDtypeStruct((n_rows * TOKEN_TILE, LANES), F32),
        compiler_params=_cparams(("arbitrary",)),
        name="experts",
    )(block_e, first, nused, tok3, tok3, dst3, dst3, h2t, w1, b1, w2, b2)


def _combine_kernel(y0_ref, y1_ref, y2_ref, y3_ref, route_ref, x_ref, g2_ref, o_ref):
    rows = x_ref.shape[0]
    route = route_ref[...]
    y = jnp.zeros(x_ref.shape, F32)
    for kk, yk_ref in enumerate((y0_ref, y1_ref, y2_ref, y3_ref)):
        yk = jnp.concatenate([yk_ref[pl.ds(s, rows, stride=TOKEN_TILE), :] for s in range(TOKEN_TILE)], axis=1)
        y = y + route[:, TOP_K + kk:TOP_K + kk + 1] * yk
    o_ref[...] = x_ref[...] + g2_ref[0] * y


def _combine(ybuf, route, x1, g2, seq):
    N, D = x1.shape
    rows = min(TOK_ROWS, N)
    steps = N // rows
    per_seq = seq // rows
    y_spec = lambda kk: pl.BlockSpec((rows * TOKEN_TILE, LANES), lambda i: (kk * steps + i, 0))
    return pl.pallas_call(
        _combine_kernel,
        grid=(steps,),
        in_specs=[y_spec(kk) for kk in range(TOP_K)] + [
            pl.BlockSpec((rows, LANES), lambda i: (i, 0)),
            pl.BlockSpec((rows, D), lambda i: (i, 0)),
            pl.BlockSpec((1, 1, D), lambda i: (i // per_seq, 0, 0)),
        ],
        out_specs=pl.BlockSpec((rows, D), lambda i: (i, 0)),
        out_shape=jax.ShapeDtypeStruct((N, D), F32),
        compiler_params=_cparams(("arbitrary",)),
        name="combine",
    )(ybuf, ybuf, ybuf, ybuf, route, x1, g2)


def _slots(route, counts, n_tokens):
    idx = route[:, :TOP_K].astype(jnp.int32)
    rank = route[:, 2 * TOP_K:3 * TOP_K].astype(jnp.int32)
    counts = counts.astype(jnp.int32)
    padded = (counts + MOE_BLOCK - 1) // MOE_BLOCK * MOE_BLOCK
    pend = jnp.cumsum(padded)
    pstart = pend - padded
    experts = jnp.arange(N_EXPERTS, dtype=jnp.int32)
    dest = rank + jnp.sum(jnp.where(idx[..., None] == experts, pstart, 0), axis=-1)
    n_pairs = n_tokens * TOP_K
    n_pad = N_EXPERTS * MOE_BLOCK
    n_blocks = (n_pairs + n_pad) // MOE_BLOCK
    nused = (pend[-1] // MOE_BLOCK).astype(jnp.int32)
    blk = jnp.minimum(jnp.arange(n_blocks, dtype=jnp.int32), nused - 1)
    block_e = jnp.sum((blk[:, None] * MOE_BLOCK >= pend[None, :]).astype(jnp.int32), axis=-1)
    first = jnp.concatenate([jnp.ones((1,), jnp.int32), (block_e[1:] != block_e[:-1]).astype(jnp.int32)])

    j = jnp.arange(MOE_BLOCK, dtype=jnp.int32)
    pad_key = jnp.where(j[None, :] < (padded - counts)[:, None], (pstart + counts)[:, None] + j[None, :], jnp.int32(2 ** 30))
    keys = jnp.concatenate([dest.reshape(-1), pad_key.reshape(-1)])
    ids = jnp.arange(n_pairs + n_pad, dtype=jnp.int32)
    _, src = lax.sort((keys, ids), num_keys=1)
    real = src < n_pairs
    tok = jnp.where(real, src // TOP_K, 0)
    dst = jnp.where(real, (src % TOP_K) * n_tokens + src // TOP_K, src)
    spare = n_pairs + n_pad + j
    tok3 = tok.reshape(n_blocks, 1, MOE_BLOCK)
    dst3 = jnp.concatenate([dst, spare]).reshape(n_blocks + 1, 1, MOE_BLOCK)
    n_rows = n_pairs + n_pad + MOE_BLOCK
    return block_e, first, nused.reshape(1), tok3, dst3, n_rows


def kernel(x, c, norm1_g, w_mod, b_mod, w_in, q_norm_g, k_norm_g, rel_bias, pool_w, pool_b, pool_scale, w_out,
           norm2_g, router_w, router_b, exp_w1, exp_b1, exp_w2, exp_b2):
    B, S, D = x.shape
    L = w_mod.shape[0]
    N = B * S
    E = router_w.shape[-1]
    row3 = lambda a: a.reshape(L, 1, -1)
    mod = _modulation(c, w_mod, b_mod)
    w_in_b, w_out_b, pool_w_b, router_w_b = (a.astype(BF16) for a in (w_in, w_out, pool_w, router_w))
    qg = row3(jnp.tile(q_norm_g, (1, N_HEADS)))
    kg = row3(jnp.tile(k_norm_g, (1, N_HEADS)))
    n1, n2, pb, ps, rb = row3(norm1_g), row3(norm2_g), row3(pool_b), row3(pool_scale), row3(router_b)
    b1 = exp_b1.reshape(L, E, 1, -1)
    b2 = exp_b2.reshape(L, E, 1, -1)
    for l in range(L):
        sh1, sc1, g1, sh2, sc2, g2 = [mod[l, :, i * D:(i + 1) * D].reshape(B, 1, D) for i in range(6)]
        q, k, v, d = _mixer_in(l, x, sh1, sc1, n1, w_in_b, qg, kg)
        attn = _attention(q, k, v, _bias_diagonals(rel_bias[l]))
        x1, h2t, route, counts = _mixer_out(l, x, attn, d, pool_w_b, pb, ps, w_out_b, g1, n2, sh2, sc2, router_w_b, rb)
        route = route.reshape(N, LANES)
        block_e, first, nused, tok3, dst3, n_rows = _slots(route, counts[0], N)
        ybuf = _experts(l, h2t, exp_w1, b1, exp_w2, b2, block_e, first, nused, tok3, dst3, n_rows,
                        dump_start=N * TOP_K, n_dump=N_EXPERTS)
        x = _combine(ybuf, route, x1.reshape(N, D), g2, S).reshape(B, S, D)
    return x
```

```python
import functools

import jax
import jax.numpy as jnp
from jax import lax
from jax.experimental import pallas as pl
from jax.experimental.pallas import tpu as pltpu

F32 = jnp.float32
BF16 = jnp.bfloat16

CHUNK = 64
LEFT_CHUNKS = 8
BAND = LEFT_CHUNKS + 1
HEAD_DIM = 64
N_HEADS = 8
ATTN_WIDTH = N_HEADS * HEAD_DIM
MAX_REL = 2 * CHUNK
POOL_WINDOWS = (2, 4, 8, 16)
POOL_GROUP = 128
POOL_WIDTH = POOL_GROUP * len(POOL_WINDOWS)
N_EXPERTS = 32
TOP_K = 4
SWIGLU_ALPHA = 1.702
SWIGLU_LIMIT = 7.0
MOE_BLOCK = 256
NORM_EPS = 1e-6
MASK_VALUE = -1e30

LANES = 128
TOKEN_TILE = 8
POOL_HALO = 16

IN_ROWS = 512
ATTN_CHUNKS = 4
OUT_ROWS = 256
TOK_ROWS = 256
VMEM_LIMIT = 56 * 1024 * 1024


def _cparams(sem, vmem=VMEM_LIMIT):
    return pltpu.CompilerParams(dimension_semantics=sem, vmem_limit_bytes=vmem)


def _layer_spec(a, l):
    zeros = (0,) * (a.ndim - 1)
    return pl.BlockSpec((1,) + a.shape[1:], lambda *_: (l,) + zeros)


def _mod_kernel(c_ref, w_ref, b_ref, o_ref):
    c = c_ref[...]
    cond = c * jax.nn.sigmoid(c)
    o_ref[0] = jnp.dot(cond.astype(BF16), w_ref[0].astype(BF16), preferred_element_type=F32) + b_ref[0]


def _modulation(c, w_mod, b_mod):
    L, D, W = w_mod.shape
    B = c.shape[0]
    tn = 1024
    return pl.pallas_call(
        _mod_kernel,
        grid=(L, W // tn),
        in_specs=[
            pl.BlockSpec((B, D), lambda l, j: (0, 0)),
            pl.BlockSpec((1, D, tn), lambda l, j: (l, 0, j)),
            pl.BlockSpec((1, 1, tn), lambda l, j: (l, 0, j)),
        ],
        out_specs=pl.BlockSpec((1, B, tn), lambda l, j: (l, 0, j)),
        out_shape=jax.ShapeDtypeStruct((L, B, W), F32),
        compiler_params=_cparams(("arbitrary", "arbitrary")),
        name="modulation",
    )(c, w_mod, b_mod.reshape(L, 1, W))


def _head_norm(t, gain):
    rows = t.shape[0]
    lane = lax.broadcasted_iota(jnp.int32, (rows, LANES), 1)
    first = lane < HEAD_DIM
    outs = []
    for p in range(ATTN_WIDTH // LANES):
        blk = t[:, p * LANES:(p + 1) * LANES]
        sq = blk * blk
        sa = jnp.sum(jnp.where(first, sq, 0.0), axis=-1, keepdims=True)
        sb = jnp.sum(jnp.where(first, 0.0, sq), axis=-1, keepdims=True)
        ra = lax.rsqrt(sa * (1.0 / HEAD_DIM) + NORM_EPS)
        rb = lax.rsqrt(sb * (1.0 / HEAD_DIM) + NORM_EPS)
        outs.append(blk * jnp.where(first, ra, rb))
    return jnp.concatenate(outs, axis=-1) * gain


def _mixer_in_kernel(x_ref, sh_ref, sc_ref, g_ref, w_ref, qg_ref, kg_ref,
                     q_ref, k_ref, v_ref, d_ref, halo_ref):
    j = pl.program_id(1)
    rows = x_ref.shape[1]
    x = x_ref[0]
    ms = jnp.mean(x * x, axis=-1, keepdims=True)
    h = (x * lax.rsqrt(ms + NORM_EPS) * g_ref[0]) * (1.0 + sc_ref[0]) + sh_ref[0]
    z = jnp.dot(h.astype(BF16), w_ref[0], preferred_element_type=F32)
    aw = ATTN_WIDTH
    q_ref[0] = (_head_norm(z[:, :aw], qg_ref[0]) * (HEAD_DIM ** -0.5)).astype(q_ref.dtype)
    k_ref[0] = _head_norm(z[:, aw:2 * aw], kg_ref[0]).astype(k_ref.dtype)
    v_ref[0] = z[:, 2 * aw:3 * aw].astype(v_ref.dtype)

    u = z[:, 3 * aw:]

    @pl.when(j == 0)
    def _():
        halo_ref[...] = jnp.zeros_like(halo_ref)

    t_idx = j * rows + lax.broadcasted_iota(jnp.int32, (rows, 1), 0)
    for gi, w in enumerate(POOL_WINDOWS):
        ls = slice(gi * POOL_GROUP, (gi + 1) * POOL_GROUP)
        ug = u[:, ls]
        ext = jnp.concatenate([halo_ref[:, ls], ug], axis=0)
        acc = ext
        span = 1
        while span < w:
            acc = acc + pltpu.roll(acc, span, 0)
            span *= 2
        win = acc[POOL_HALO:, :]
        count = jnp.minimum(t_idx + 1, w).astype(F32)
        d_ref[0, :, ls] = (win / count - ug).astype(d_ref.dtype)
    halo_ref[...] = u[rows - POOL_HALO:, :]


def _mixer_in(l, x, sh, sc, g, w_in, qg, kg):
    B, S, D = x.shape
    rows = min(IN_ROWS, S)
    o_spec = pl.BlockSpec((1, rows, ATTN_WIDTH), lambda b, j: (b, j, 0))
    o_shape = jax.ShapeDtypeStruct((B, S, ATTN_WIDTH), BF16)
    per_b = pl.BlockSpec((1, 1, D), lambda b, j: (b, 0, 0))
    return pl.pallas_call(
        _mixer_in_kernel,
        grid=(B, S // rows),
        in_specs=[
            pl.BlockSpec((1, rows, D), lambda b, j: (b, j, 0)),
            per_b, per_b, _layer_spec(g, l), _layer_spec(w_in, l), _layer_spec(qg, l), _layer_spec(kg, l),
        ],
        out_specs=[o_spec, o_spec, o_spec, o_spec],
        out_shape=[o_shape, o_shape, o_shape, o_shape],
        scratch_shapes=[pltpu.VMEM((POOL_HALO, POOL_WIDTH), F32)],
        compiler_params=_cparams(("arbitrary", "arbitrary")),
        name="mixer_in",
    )(x, sh, sc, g, w_in, qg, kg)


def _attn_kernel(q_ref, k0_ref, k1_ref, k2_ref, v0_ref, v1_ref, v2_ref, diag_ref, o_ref, bias_ref):
    i = pl.program_id(1)
    tq = q_ref.shape[1]
    k_refs = (k0_ref, k1_ref, k2_ref)
    v_refs = (v0_ref, v1_ref, v2_ref)
    nt = len(k_refs)

    @pl.when((pl.program_id(0) == 0) & (i == 0))
    def _():
        row = lax.broadcasted_iota(jnp.int32, (tq, nt * tq), 0)
        col = lax.broadcasted_iota(jnp.int32, (tq, nt * tq), 1)
        in_band = col - (row // CHUNK) * CHUNK
        ok = (in_band >= 0) & (in_band < BAND * CHUNK)
        width = diag_ref.shape[1]
        for h in range(N_HEADS):
            table = jnp.broadcast_to(diag_ref[h:h + 1, :], (tq, width))
            toeplitz = pltpu.roll(table, width - tq, 1, stride=1, stride_axis=0)[:, :nt * tq]
            bias_ref[h // 2, (h % 2) * tq:(h % 2 + 1) * tq, :] = jnp.where(ok, toeplitz, MASK_VALUE)

    lane = lax.broadcasted_iota(jnp.int32, (tq, LANES), 1)
    first = lane < HEAD_DIM
    kpos = lax.broadcasted_iota(jnp.int32, (2 * tq, nt * tq), 1)
    in_seq = kpos >= (nt - 1 - i) * tq
    nt_dims = (((1,), (1,)), ((), ()))
    for p in range(ATTN_WIDTH // LANES):
        ls = slice(p * LANES, (p + 1) * LANES)
        q = q_ref[0, :, ls]
        zero = jnp.zeros_like(q)
        q2 = jnp.concatenate([jnp.where(first, q, zero), jnp.where(first, zero, q)], axis=0)
        s = jnp.concatenate(
            [lax.dot_general(q2, kr[0, :, ls], nt_dims, preferred_element_type=F32) for kr in k_refs], axis=1)
        s = jnp.where(in_seq, s + bias_ref[p], MASK_VALUE)
        m = jnp.max(s, axis=-1, keepdims=True)
        e = jnp.exp(s - m)
        l = jnp.sum(e, axis=-1, keepdims=True)
        eb = e.astype(BF16)
        pv = jnp.dot(eb[:, :tq], v_refs[0][0, :, ls], preferred_element_type=F32)
        for n in range(1, nt):
            pv = pv + jnp.dot(eb[:, n * tq:(n + 1) * tq], v_refs[n][0, :, ls], preferred_element_type=F32)
        pv = pv / l
        o_ref[0, :, ls] = jnp.where(first, pv[:tq], pv[tq:]).astype(o_ref.dtype)


def _attention(q, k, v, diag):
    B, S, W = q.shape
    tq = ATTN_CHUNKS * CHUNK
    assert LEFT_CHUNKS == 2 * ATTN_CHUNKS and S % tq == 0
    cur = pl.BlockSpec((1, tq, W), lambda b, i: (b, i, 0))
    back = lambda n: pl.BlockSpec((1, tq, W), lambda b, i: (b, jnp.maximum(i - n, 0), 0))
    return pl.pallas_call(
        _attn_kernel,
        grid=(B, S // tq),
        in_specs=[cur, back(2), back(1), cur, back(2), back(1), cur,
                  pl.BlockSpec(diag.shape, lambda b, i: (0, 0))],
        out_specs=cur,
        out_shape=jax.ShapeDtypeStruct((B, S, W), BF16),
        scratch_shapes=[pltpu.VMEM((N_HEADS // 2, 2 * tq, 3 * tq), F32)],
        compiler_params=_cparams(("arbitrary", "arbitrary")),
        name="attention",
    )(q, k, k, k, v, v, v, diag)


def _bias_diagonals(rel_bias):
    tq = ATTN_CHUNKS * CHUNK
    H = rel_bias.shape[0]
    lo = tq + LEFT_CHUNKS * CHUNK - MAX_REL
    hi = 4 * tq - lo - (2 * MAX_REL + 1)
    return jnp.concatenate([jnp.broadcast_to(rel_bias[:, :1], (H, lo)), rel_bias,
                            jnp.broadcast_to(rel_bias[:, -1:], (H, hi))], axis=1).astype(F32)


def _mixer_out_kernel(x_ref, a_ref, d_ref, pw_ref, pb_ref, ps_ref, wo_ref, g1_ref, n2_ref, sh_ref, sc_ref,
                      rw_ref, rb_ref, x1_ref, h2_ref, route_ref, cnt_ref, run_ref):
    step = pl.program_id(0) * pl.num_programs(1) + pl.program_id(1)
    rows = x_ref.shape[1]

    @pl.when(step == 0)
    def _():
        run_ref[...] = jnp.zeros_like(run_ref)

    y = jnp.dot(a_ref[0], wo_ref[0, :ATTN_WIDTH, :], preferred_element_type=F32)
    for gi in range(len(POOL_WINDOWS)):
        ls = slice(gi * POOL_GROUP, (gi + 1) * POOL_GROUP)
        pg = jnp.dot(d_ref[0, :, ls], pw_ref[0, gi], preferred_element_type=F32)
        pg = (pg + pb_ref[0, :, ls]) * ps_ref[0, :, ls]
        lo = ATTN_WIDTH + gi * POOL_GROUP
        y = y + jnp.dot(pg.astype(BF16), wo_ref[0, lo:lo + POOL_GROUP, :], preferred_element_type=F32)
    x1 = x_ref[0] + g1_ref[0] * y
    x1_ref[0] = x1

    ms = jnp.mean(x1 * x1, axis=-1, keepdims=True)
    h2 = (x1 * lax.rsqrt(ms + NORM_EPS) * n2_ref[0]) * (1.0 + sc_ref[0]) + sh_ref[0]
    for s in range(TOKEN_TILE):
        h2_ref[pl.ds(s, rows, stride=TOKEN_TILE), :] = h2[:, s * LANES:(s + 1) * LANES]

    logits = jnp.dot(h2.astype(BF16), rw_ref[0], preferred_element_type=F32) + rb_ref[0]
    ne = logits.shape[-1]
    elane = lax.broadcasted_iota(jnp.int32, (rows, ne), 1)
    work = logits
    picked = jnp.zeros((rows, ne), F32)
    vals, idxs = [], []
    for _ in range(TOP_K):
        m = jnp.max(work, axis=-1, keepdims=True)
        idx = jnp.min(jnp.where(work == m, elane, ne), axis=-1, keepdims=True)
        sel = elane == idx
        vals.append(m)
        idxs.append(idx)
        work = jnp.where(sel, -jnp.inf, work)
        picked = picked + sel.astype(F32)
    exps = [jnp.exp(v - vals[0]) for v in vals]
    denom = exps[0] + exps[1] + exps[2] + exps[3]

    r_i = lax.broadcasted_iota(jnp.int32, (rows, rows), 0)
    c_i = lax.broadcasted_iota(jnp.int32, (rows, rows), 1)
    tri = (c_i < r_i).astype(BF16)
    before = jnp.dot(tri, picked.astype(BF16), preferred_element_type=F32) + run_ref[...]
    run_ref[...] = run_ref[...] + jnp.sum(picked, axis=0, keepdims=True)

    lane = lax.broadcasted_iota(jnp.int32, (rows, LANES), 1)
    route = jnp.zeros((rows, LANES), F32)
    for kk in range(TOP_K):
        rank = jnp.sum(jnp.where(elane == idxs[kk], before, 0.0), axis=-1, keepdims=True)
        route = jnp.where(lane == kk, idxs[kk].astype(F32), route)
        route = jnp.where(lane == TOP_K + kk, exps[kk] / denom, route)
        route = jnp.where(lane == 2 * TOP_K + kk, rank, route)
    route_ref[0] = route
    cnt_ref[...] = jnp.broadcast_to(run_ref[...], cnt_ref.shape)


def _mixer_out(l, x, attn, d, pool_w, pool_b, pool_scale, w_out, g1, n2, sh2, sc2, router_w, router_b):
    B, S, D = x.shape
    rows = min(OUT_ROWS, S)
    E = router_w.shape[-1]
    tile = lambda w: pl.BlockSpec((1, rows, w), lambda b, j: (b, j, 0))
    per_b = pl.BlockSpec((1, 1, D), lambda b, j: (b, 0, 0))
    lay = lambda a: _layer_spec(a, l)
    return pl.pallas_call(
        _mixer_out_kernel,
        grid=(B, S // rows),
        in_specs=[
            tile(D), tile(ATTN_WIDTH), tile(POOL_WIDTH),
            lay(pool_w), lay(pool_b), lay(pool_scale), lay(w_out),
            per_b, lay(n2), per_b, per_b, lay(router_w), lay(router_b),
        ],
        out_specs=[tile(D), pl.BlockSpec((rows * TOKEN_TILE, LANES), lambda b, j: (b * (S // rows) + j, 0)),
                   tile(LANES), pl.BlockSpec((8, E), lambda b, j: (0, 0))],
        out_shape=[
            jax.ShapeDtypeStruct((B, S, D), F32),
            jax.ShapeDtypeStruct((B * S * TOKEN_TILE, LANES), F32),
            jax.ShapeDtypeStruct((B, S, LANES), F32),
            jax.ShapeDtypeStruct((8, E), F32),
        ],
        scratch_shapes=[pltpu.VMEM((1, E), F32)],
        compiler_params=_cparams(("arbitrary", "arbitrary")),
        name="mixer_out",
    )(x, attn, d, pool_w, pool_b, pool_scale, w_out, g1, n2, sh2, sc2, router_w, router_b)


def _experts_kernel(be_ref, first_ref, nused_ref, tok_cur_ref, tok_n1_ref, tok_n2_ref, dst_prev_ref, dst_cur_ref,
                    h_ref, w1_ref, b1_ref, w2_ref, b2_ref, y_ref,
                    x0_ref, x1_ref, x2_ref, o0_ref, o1_ref, o2_ref, w1s_ref, w2s_ref, gsem, ssem, zsem,
                    *, dump_start, n_dump):
    i = pl.program_id(0)
    nused = nused_ref[0]
    rows = MOE_BLOCK
    tr = TOKEN_TILE
    f = w2_ref.shape[2]

    def tile_in(tok, r, xbuf, sem):
        return pltpu.make_async_copy(h_ref.at[pl.ds(pl.multiple_of(tok * tr, tr), tr)], xbuf.at[pl.ds(r * tr, tr)], sem)

    def tile_out(obuf, r, dst, sem):
        return pltpu.make_async_copy(obuf.at[pl.ds(r * tr, tr)], y_ref.at[pl.ds(pl.multiple_of(dst * tr, tr), tr)], sem)

    def wait_in(xbuf, sem):
        pltpu.make_async_copy(h_ref.at[pl.ds(0, rows * tr)], xbuf, sem).wait()

    def wait_out(obuf, sem):
        pltpu.make_async_copy(obuf, y_ref.at[pl.ds(0, rows * tr)], sem).wait()

    xs = (x0_ref, x1_ref, x2_ref)
    os_ = (o0_ref, o1_ref, o2_ref)

    def gather_loop(tok_ref, xbuf, sem):
        def body(r, carry):
            tile_in(tok_ref[0, 0, r], r, xbuf, sem).start()
            return carry
        lax.fori_loop(0, rows, body, 0, unroll=8)

    def step(m):
        p1, p2 = (m + 2) % 3, (m + 1) % 3
        xcur, ocur, oprev = xs[m], os_[m], os_[p1]

        if m == 0:
            @pl.when(i == 0)
            def _():
                o0_ref[...] = jnp.zeros_like(o0_ref)
                o2_ref[...] = jnp.zeros_like(o2_ref)
                for c in range(n_dump):
                    pltpu.make_async_copy(o0_ref, y_ref.at[pl.ds((dump_start + c * rows) * tr, rows * tr)], zsem).start()
                for c in range(n_dump):
                    pltpu.make_async_copy(o0_ref, y_ref.at[pl.ds((dump_start + c * rows) * tr, rows * tr)], zsem).wait()
                gather_loop(tok_cur_ref, x0_ref, gsem.at[0])
                gather_loop(tok_n1_ref, x1_ref, gsem.at[1])

        wait_in(xcur, gsem.at[m])

        @pl.when(i >= 2)
        def _():
            wait_out(ocur, ssem.at[m])

        @pl.when(first_ref[i] == 1)
        def _():
            w1s_ref[...] = w1_ref[0, 0].astype(BF16)
            w2s_ref[...] = w2_ref[0, 0].astype(BF16)

        for r in range(rows):
            tile_out(oprev, r, dst_prev_ref[0, 0, r], ssem.at[p1]).start()
        for r in range(rows):
            tile_in(tok_n2_ref[0, 0, r], r, xs[p1], gsem.at[p1]).start()
        x = jnp.concatenate([xcur[pl.ds(s, rows, stride=tr), :] for s in range(tr)], axis=1).astype(BF16)
        gu = jnp.dot(x, w1s_ref[...], preferred_element_type=F32) + b1_ref[0, 0]
        glu = jnp.minimum(gu[:, :f], SWIGLU_LIMIT)
        lin = jnp.clip(gu[:, f:], -SWIGLU_LIMIT, SWIGLU_LIMIT)
        act = glu * jax.nn.sigmoid(SWIGLU_ALPHA * glu) * (lin + 1.0)
        o = jnp.dot(act.astype(BF16), w2s_ref[...], preferred_element_type=F32) + b2_ref[0, 0]
        for s in range(tr):
            ocur[pl.ds(s, rows, stride=tr), :] = o[:, s * LANES:(s + 1) * LANES]

        @pl.when(i == nused - 1)
        def _():
            def last_scatter(r, carry):
                tile_out(ocur, r, dst_cur_ref[0, 0, r], ssem.at[m]).start()
                return carry
            lax.fori_loop(0, rows, last_scatter, 0, unroll=8)
            wait_out(os_[p2], ssem.at[p2])
            wait_out(oprev, ssem.at[p1])
            wait_out(ocur, ssem.at[m])
            wait_in(xs[p2], gsem.at[p2])
            wait_in(xs[p1], gsem.at[p1])

    for m in range(3):
        @pl.when((i < nused) & (i % 3 == m))
        def _(m=m):
            step(m)


def _experts(l, h2t, w1, b1, w2, b2, block_e, first, nused, tok3, dst3, n_rows, dump_start, n_dump):
    D, F2 = w1.shape[2:]
    F = w2.shape[2]
    nb = tok3.shape[0]
    idx_blk = (1, 1, MOE_BLOCK)
    smem = lambda fn: pl.BlockSpec(idx_blk, fn, memory_space=pltpu.SMEM)
    by_e = lambda i, be, fi, nu: (l, be[i], 0, 0)
    block_rows = MOE_BLOCK * TOKEN_TILE
    grid_spec = pltpu.PrefetchScalarGridSpec(
        num_scalar_prefetch=3,
        grid=(nb,),
        in_specs=[
            smem(lambda i, be, fi, nu: (i, 0, 0)),
            smem(lambda i, be, fi, nu: (jnp.minimum(i + 1, nb - 1), 0, 0)),
            smem(lambda i, be, fi, nu: (jnp.minimum(i + 2, nb - 1), 0, 0)),
            smem(lambda i, be, fi, nu: (jnp.where(i == 0, nb, i - 1), 0, 0)),
            smem(lambda i, be, fi, nu: (i, 0, 0)),
            pl.BlockSpec(memory_space=pl.ANY),
            pl.BlockSpec((1, 1, D, F2), by_e),
            pl.BlockSpec((1, 1, 1, F2), by_e),
            pl.BlockSpec((1, 1, F, D), by_e),
            pl.BlockSpec((1, 1, 1, D), by_e),
        ],
        out_specs=pl.BlockSpec(memory_space=pl.ANY),
        scratch_shapes=[pltpu.VMEM((block_rows, LANES), F32)] * 6 + [
            pltpu.VMEM((D, F2), BF16), pltpu.VMEM((F, D), BF16),
            pltpu.SemaphoreType.DMA((3,)), pltpu.SemaphoreType.DMA((3,)), pltpu.SemaphoreType.DMA(())],
    )
    return pl.pallas_call(
        functools.partial(_experts_kernel, dump_start=dump_start, n_dump=n_dump),
        grid_spec=grid_spec,
        out_shape=jax.ShapeDtypeStruct((n_rows * TOKEN_TILE, LANES), F32),
        compiler_params=_cparams(("arbitrary",)),
        name="experts",
    )(block_e, first, nused, tok3, tok3, tok3, dst3, dst3, h2t, w1, b1, w2, b2)


def _combine_kernel(y0_ref, y1_ref, y2_ref, y3_ref, route_ref, x_ref, g2_ref, o_ref):
    rows = x_ref.shape[0]
    route = route_ref[...]
    y = jnp.zeros(x_ref.shape, F32)
    for kk, yk_ref in enumerate((y0_ref, y1_ref, y2_ref, y3_ref)):
        yk = jnp.concatenate([yk_ref[pl.ds(s, rows, stride=TOKEN_TILE), :] for s in range(TOKEN_TILE)], axis=1)
        y = y + route[:, TOP_K + kk:TOP_K + kk + 1] * yk
    o_ref[...] = x_ref[...] + g2_ref[0] * y


def _combine(ybuf, route, x1, g2, seq):
    N, D = x1.shape
    rows = min(TOK_ROWS, N)
    steps = N // rows
    per_seq = seq // rows
    y_spec = lambda kk: pl.BlockSpec((rows * TOKEN_TILE, LANES), lambda i: (kk * steps + i, 0))
    return pl.pallas_call(
        _combine_kernel,
        grid=(steps,),
        in_specs=[y_spec(kk) for kk in range(TOP_K)] + [
            pl.BlockSpec((rows, LANES), lambda i: (i, 0)),
            pl.BlockSpec((rows, D), lambda i: (i, 0)),
            pl.BlockSpec((1, 1, D), lambda i: (i // per_seq, 0, 0)),
        ],
        out_specs=pl.BlockSpec((rows, D), lambda i: (i, 0)),
        out_shape=jax.ShapeDtypeStruct((N, D), F32),
        compiler_params=_cparams(("arbitrary",)),
        name="combine",
    )(ybuf, ybuf, ybuf, ybuf, route, x1, g2)


def _slots(route, counts, n_tokens):
    idx = route[:, :TOP_K].astype(jnp.int32)
    rank = route[:, 2 * TOP_K:3 * TOP_K].astype(jnp.int32)
    counts = counts.astype(jnp.int32)
    padded = (counts + MOE_BLOCK - 1) // MOE_BLOCK * MOE_BLOCK
    pend = jnp.cumsum(padded)
    pstart = pend - padded
    experts = jnp.arange(N_EXPERTS, dtype=jnp.int32)
    dest = rank + jnp.sum(jnp.where(idx[..., None] == experts, pstart, 0), axis=-1)
    n_pairs = n_tokens * TOP_K
    n_pad = N_EXPERTS * MOE_BLOCK
    n_blocks = (n_pairs + n_pad) // MOE_BLOCK
    nused = (pend[-1] // MOE_BLOCK).astype(jnp.int32)
    blk = jnp.minimum(jnp.arange(n_blocks, dtype=jnp.int32), nused - 1)
    block_e = jnp.sum((blk[:, None] * MOE_BLOCK >= pend[None, :]).astype(jnp.int32), axis=-1)
    first = jnp.concatenate([jnp.ones((1,), jnp.int32), (block_e[1:] != block_e[:-1]).astype(jnp.int32)])

    j = jnp.arange(MOE_BLOCK, dtype=jnp.int32)
    pad_key = jnp.where(j[None, :] < (padded - counts)[:, None], (pstart + counts)[:, None] + j[None, :], jnp.int32(2 ** 30))
    keys = jnp.concatenate([dest.reshape(-1), pad_key.reshape(-1)])
    ids = jnp.arange(n_pairs + n_pad, dtype=jnp.int32)
    _, src = lax.sort((keys, ids), num_keys=1)
    real = src < n_pairs
    tok = jnp.where(real, src // TOP_K, 0)
    dst = jnp.where(real, (src % TOP_K) * n_tokens + src // TOP_K, src)
    spare = n_pairs + n_pad + j
    tok3 = tok.reshape(n_blocks, 1, MOE_BLOCK)
    dst3 = jnp.concatenate([dst, spare]).reshape(n_blocks + 1, 1, MOE_BLOCK)
    n_rows = n_pairs + n_pad + MOE_BLOCK
    return block_e, first, nused.reshape(1), tok3, dst3, n_rows


def kernel(x, c, norm1_g, w_mod, b_mod, w_in, q_norm_g, k_norm_g, rel_bias, pool_w, pool_b, pool_scale, w_out,
           norm2_g, router_w, router_b, exp_w1, exp_b1, exp_w2, exp_b2):
    B, S, D = x.shape
    L = w_mod.shape[0]
    N = B * S
    E = router_w.shape[-1]
    row3 = lambda a: a.reshape(L, 1, -1)
    mod = _modulation(c, w_mod, b_mod)
    w_in_b, w_out_b, pool_w_b, router_w_b = (a.astype(BF16) for a in (w_in, w_out, pool_w, router_w))
    qg = row3(jnp.tile(q_norm_g, (1, N_HEADS)))
    kg = row3(jnp.tile(k_norm_g, (1, N_HEADS)))
    n1, n2, pb, ps, rb = row3(norm1_g), row3(norm2_g), row3(pool_b), row3(pool_scale), row3(router_b)
    b1 = exp_b1.reshape(L, E, 1, -1)
    b2 = exp_b2.reshape(L, E, 1, -1)
    for l in range(L):
        sh1, sc1, g1, sh2, sc2, g2 = [mod[l, :, i * D:(i + 1) * D].reshape(B, 1, D) for i in range(6)]
        q, k, v, d = _mixer_in(l, x, sh1, sc1, n1, w_in_b, qg, kg)
        attn = _attention(q, k, v, _bias_diagonals(rel_bias[l]))
        x1, h2t, route, counts = _mixer_out(l, x, attn, d, pool_w_b, pb, ps, w_out_b, g1, n2, sh2, sc2, router_w_b, rb)
        route = route.reshape(N, LANES)
        block_e, first, nused, tok3, dst3, n_rows = _slots(route, counts[0], N)
        ybuf = _experts(l, h2t, exp_w1, b1, exp_w2, b2, block_e, first, nused, tok3, dst3, n_rows,
                        dump_start=N * TOP_K, n_dump=N_EXPERTS)
        x = _combine(ybuf, route, x1.reshape(N, D), g2, S).reshape(B, S, D)
    return x
```

```python
import functools

import jax
import jax.numpy as jnp
from jax import lax
from jax.experimental import pallas as pl
from jax.experimental.pallas import tpu as pltpu

F32 = jnp.float32
BF16 = jnp.bfloat16

CHUNK = 64
LEFT_CHUNKS = 8
BAND = LEFT_CHUNKS + 1
HEAD_DIM = 64
N_HEADS = 8
ATTN_WIDTH = N_HEADS * HEAD_DIM
MAX_REL = 2 * CHUNK
POOL_WINDOWS = (2, 4, 8, 16)
POOL_GROUP = 128
POOL_WIDTH = POOL_GROUP * len(POOL_WINDOWS)
N_EXPERTS = 32
TOP_K = 4
SWIGLU_ALPHA = 1.702
SWIGLU_LIMIT = 7.0
MOE_BLOCK = 256
NORM_EPS = 1e-6
MASK_VALUE = -1e30

LANES = 128
TOKEN_TILE = 8
POOL_HALO = 16

IN_ROWS = 512
ATTN_CHUNKS = 4
OUT_ROWS = 512
TOK_ROWS = 256
VMEM_LIMIT = 56 * 1024 * 1024


def _cparams(sem, vmem=VMEM_LIMIT):
    return pltpu.CompilerParams(dimension_semantics=sem, vmem_limit_bytes=vmem)


def _layer_spec(a, l):
    zeros = (0,) * (a.ndim - 1)
    return pl.BlockSpec((1,) + a.shape[1:], lambda *_: (l,) + zeros)


def _mod_kernel(c_ref, w_ref, b_ref, o_ref):
    c = c_ref[...]
    cond = c * jax.nn.sigmoid(c)
    o_ref[0] = jnp.dot(cond.astype(BF16), w_ref[0].astype(BF16), preferred_element_type=F32) + b_ref[0]


def _modulation(c, w_mod, b_mod):
    L, D, W = w_mod.shape
    B = c.shape[0]
    tn = 1024
    return pl.pallas_call(
        _mod_kernel,
        grid=(L, W // tn),
        in_specs=[
            pl.BlockSpec((B, D), lambda l, j: (0, 0)),
            pl.BlockSpec((1, D, tn), lambda l, j: (l, 0, j)),
            pl.BlockSpec((1, 1, tn), lambda l, j: (l, 0, j)),
        ],
        out_specs=pl.BlockSpec((1, B, tn), lambda l, j: (l, 0, j)),
        out_shape=jax.ShapeDtypeStruct((L, B, W), F32),
        compiler_params=_cparams(("arbitrary", "arbitrary")),
        name="modulation",
    )(c, w_mod, b_mod.reshape(L, 1, W))


def _head_norm(t, gain):
    rows = t.shape[0]
    lane = lax.broadcasted_iota(jnp.int32, (rows, LANES), 1)
    first = lane < HEAD_DIM
    outs = []
    for p in range(ATTN_WIDTH // LANES):
        blk = t[:, p * LANES:(p + 1) * LANES]
        sq = blk * blk
        sa = jnp.sum(jnp.where(first, sq, 0.0), axis=-1, keepdims=True)
        sb = jnp.sum(jnp.where(first, 0.0, sq), axis=-1, keepdims=True)
        ra = lax.rsqrt(sa * (1.0 / HEAD_DIM) + NORM_EPS)
        rb = lax.rsqrt(sb * (1.0 / HEAD_DIM) + NORM_EPS)
        outs.append(blk * jnp.where(first, ra, rb))
    return jnp.concatenate(outs, axis=-1) * gain


def _mixer_in_kernel(x_ref, sh_ref, sc_ref, g_ref, w_ref, qg_ref, kg_ref,
                     q_ref, k_ref, v_ref, d_ref, halo_ref):
    j = pl.program_id(1)
    rows = x_ref.shape[1]
    x = x_ref[0]
    ms = jnp.mean(x * x, axis=-1, keepdims=True)
    h = (x * lax.rsqrt(ms + NORM_EPS) * g_ref[0]) * (1.0 + sc_ref[0]) + sh_ref[0]
    z = jnp.dot(h.astype(BF16), w_ref[0], preferred_element_type=F32)
    aw = ATTN_WIDTH
    q_ref[0] = (_head_norm(z[:, :aw], qg_ref[0]) * (HEAD_DIM ** -0.5)).astype(q_ref.dtype)
    k_ref[0] = _head_norm(z[:, aw:2 * aw], kg_ref[0]).astype(k_ref.dtype)
    v_ref[0] = z[:, 2 * aw:3 * aw].astype(v_ref.dtype)

    u = z[:, 3 * aw:]

    @pl.when(j == 0)
    def _():
        halo_ref[...] = jnp.zeros_like(halo_ref)

    t_idx = j * rows + lax.broadcasted_iota(jnp.int32, (rows, 1), 0)
    for gi, w in enumerate(POOL_WINDOWS):
        ls = slice(gi * POOL_GROUP, (gi + 1) * POOL_GROUP)
        ug = u[:, ls]
        ext = jnp.concatenate([halo_ref[:, ls], ug], axis=0)
        acc = ext
        span = 1
        while span < w:
            acc = acc + pltpu.roll(acc, span, 0)
            span *= 2
        win = acc[POOL_HALO:, :]
        count = jnp.minimum(t_idx + 1, w).astype(F32)
        d_ref[0, :, ls] = (win / count - ug).astype(d_ref.dtype)
    halo_ref[...] = u[rows - POOL_HALO:, :]


def _mixer_in(l, x, sh, sc, g, w_in, qg, kg):
    B, S, D = x.shape
    rows = min(IN_ROWS, S)
    o_spec = pl.BlockSpec((1, rows, ATTN_WIDTH), lambda b, j: (b, j, 0))
    o_shape = jax.ShapeDtypeStruct((B, S, ATTN_WIDTH), BF16)
    per_b = pl.BlockSpec((1, 1, D), lambda b, j: (b, 0, 0))
    return pl.pallas_call(
        _mixer_in_kernel,
        grid=(B, S // rows),
        in_specs=[
            pl.BlockSpec((1, rows, D), lambda b, j: (b, j, 0)),
            per_b, per_b, _layer_spec(g, l), _layer_spec(w_in, l), _layer_spec(qg, l), _layer_spec(kg, l),
        ],
        out_specs=[o_spec, o_spec, o_spec, o_spec],
        out_shape=[o_shape, o_shape, o_shape, o_shape],
        scratch_shapes=[pltpu.VMEM((POOL_HALO, POOL_WIDTH), F32)],
        compiler_params=_cparams(("arbitrary", "arbitrary")),
        name="mixer_in",
    )(x, sh, sc, g, w_in, qg, kg)


def _attn_kernel(q_ref, k0_ref, k1_ref, k2_ref, v0_ref, v1_ref, v2_ref, diag_ref, o_ref, bias_ref):
    i = pl.program_id(1)
    tq = q_ref.shape[1]
    k_refs = (k0_ref, k1_ref, k2_ref)
    v_refs = (v0_ref, v1_ref, v2_ref)
    nt = len(k_refs)

    @pl.when((pl.program_id(0) == 0) & (i == 0))
    def _():
        row = lax.broadcasted_iota(jnp.int32, (tq, nt * tq), 0)
        col = lax.broadcasted_iota(jnp.int32, (tq, nt * tq), 1)
        in_band = col - (row // CHUNK) * CHUNK
        ok = (in_band >= 0) & (in_band < BAND * CHUNK)
        width = diag_ref.shape[1]
        for h in range(N_HEADS):
            table = jnp.broadcast_to(diag_ref[h:h + 1, :], (tq, width))
            toeplitz = pltpu.roll(table, width - tq, 1, stride=1, stride_axis=0)[:, :nt * tq]
            bias_ref[h // 2, (h % 2) * tq:(h % 2 + 1) * tq, :] = jnp.where(ok, toeplitz, MASK_VALUE)

    lane = lax.broadcasted_iota(jnp.int32, (tq, LANES), 1)
    first = lane < HEAD_DIM
    kpos = lax.broadcasted_iota(jnp.int32, (2 * tq, nt * tq), 1)
    in_seq = kpos >= (nt - 1 - i) * tq
    nt_dims = (((1,), (1,)), ((), ()))
    for p in range(ATTN_WIDTH // LANES):
        ls = slice(p * LANES, (p + 1) * LANES)
        q = q_ref[0, :, ls]
        zero = jnp.zeros_like(q)
        q2 = jnp.concatenate([jnp.where(first, q, zero), jnp.where(first, zero, q)], axis=0)
        s = jnp.concatenate(
            [lax.dot_general(q2, kr[0, :, ls], nt_dims, preferred_element_type=F32) for kr in k_refs], axis=1)
        s = jnp.where(in_seq, s + bias_ref[p], MASK_VALUE)
        m = jnp.max(s, axis=-1, keepdims=True)
        e = jnp.exp(s - m)
        l = jnp.sum(e, axis=-1, keepdims=True)
        eb = e.astype(BF16)
        pv = jnp.dot(eb[:, :tq], v_refs[0][0, :, ls], preferred_element_type=F32)
        for n in range(1, nt):
            pv = pv + jnp.dot(eb[:, n * tq:(n + 1) * tq], v_refs[n][0, :, ls], preferred_element_type=F32)
        pv = pv / l
        o_ref[0, :, ls] = jnp.where(first, pv[:tq], pv[tq:]).astype(o_ref.dtype)


def _attention(q, k, v, diag):
    B, S, W = q.shape
    tq = ATTN_CHUNKS * CHUNK
    assert LEFT_CHUNKS == 2 * ATTN_CHUNKS and S % tq == 0
    cur = pl.BlockSpec((1, tq, W), lambda b, i: (b, i, 0))
    back = lambda n: pl.BlockSpec((1, tq, W), lambda b, i: (b, jnp.maximum(i - n, 0), 0))
    return pl.pallas_call(
        _attn_kernel,
        grid=(B, S // tq),
        in_specs=[cur, back(2), back(1), cur, back(2), back(1), cur,
                  pl.BlockSpec(diag.shape, lambda b, i: (0, 0))],
        out_specs=cur,
        out_shape=jax.ShapeDtypeStruct((B, S, W), BF16),
        scratch_shapes=[pltpu.VMEM((N_HEADS // 2, 2 * tq, 3 * tq), F32)],
        compiler_params=_cparams(("arbitrary", "arbitrary")),
        name="attention",
    )(q, k, k, k, v, v, v, diag)


def _bias_diagonals(rel_bias):
    tq = ATTN_CHUNKS * CHUNK
    H = rel_bias.shape[0]
    lo = tq + LEFT_CHUNKS * CHUNK - MAX_REL
    hi = 4 * tq - lo - (2 * MAX_REL + 1)
    return jnp.concatenate([jnp.broadcast_to(rel_bias[:, :1], (H, lo)), rel_bias,
                            jnp.broadcast_to(rel_bias[:, -1:], (H, hi))], axis=1).astype(F32)


def _mixer_out_kernel(x_ref, a_ref, d_ref, pw_ref, pb_ref, ps_ref, wo_ref, g1_ref, n2_ref, sh_ref, sc_ref,
                      rw_ref, rb_ref, x1_ref, h2_ref, route_ref, cnt_ref, run_ref):
    step = pl.program_id(0) * pl.num_programs(1) + pl.program_id(1)
    rows = x_ref.shape[1]

    @pl.when(step == 0)
    def _():
        run_ref[...] = jnp.zeros_like(run_ref)

    y = jnp.dot(a_ref[0], wo_ref[0, :ATTN_WIDTH, :], preferred_element_type=F32)
    for gi in range(len(POOL_WINDOWS)):
        ls = slice(gi * POOL_GROUP, (gi + 1) * POOL_GROUP)
        pg = jnp.dot(d_ref[0, :, ls], pw_ref[0, gi], preferred_element_type=F32)
        pg = (pg + pb_ref[0, :, ls]) * ps_ref[0, :, ls]
        lo = ATTN_WIDTH + gi * POOL_GROUP
        y = y + jnp.dot(pg.astype(BF16), wo_ref[0, lo:lo + POOL_GROUP, :], preferred_element_type=F32)
    x1 = x_ref[0] + g1_ref[0] * y
    x1_ref[0] = x1

    ms = jnp.mean(x1 * x1, axis=-1, keepdims=True)
    h2 = (x1 * lax.rsqrt(ms + NORM_EPS) * n2_ref[0]) * (1.0 + sc_ref[0]) + sh_ref[0]
    for s in range(TOKEN_TILE):
        h2_ref[pl.ds(s, rows, stride=TOKEN_TILE), :] = h2[:, s * LANES:(s + 1) * LANES]

    logits = jnp.dot(h2.astype(BF16), rw_ref[0], preferred_element_type=F32) + rb_ref[0]
    ne = logits.shape[-1]
    elane = lax.broadcasted_iota(jnp.int32, (rows, ne), 1)
    work = logits
    picked = jnp.zeros((rows, ne), F32)
    vals, idxs = [], []
    for _ in range(TOP_K):
        m = jnp.max(work, axis=-1, keepdims=True)
        idx = jnp.min(jnp.where(work == m, elane, ne), axis=-1, keepdims=True)
        sel = elane == idx
        vals.append(m)
        idxs.append(idx)
        work = jnp.where(sel, -jnp.inf, work)
        picked = picked + sel.astype(F32)
    exps = [jnp.exp(v - vals[0]) for v in vals]
    denom = exps[0] + exps[1] + exps[2] + exps[3]

    r_i = lax.broadcasted_iota(jnp.int32, (rows, rows), 0)
    c_i = lax.broadcasted_iota(jnp.int32, (rows, rows), 1)
    tri = (c_i < r_i).astype(BF16)
    before = jnp.dot(tri, picked.astype(BF16), preferred_element_type=F32) + run_ref[...]
    run_ref[...] = run_ref[...] + jnp.sum(picked, axis=0, keepdims=True)

    lane = lax.broadcasted_iota(jnp.int32, (rows, LANES), 1)
    route = jnp.zeros((rows, LANES), F32)
    for kk in range(TOP_K):
        rank = jnp.sum(jnp.where(elane == idxs[kk], before, 0.0), axis=-1, keepdims=True)
        route = jnp.where(lane == kk, idxs[kk].astype(F32), route)
        route = jnp.where(lane == TOP_K + kk, exps[kk] / denom, route)
        route = jnp.where(lane == 2 * TOP_K + kk, rank, route)
    route_ref[0] = route
    cnt_ref[...] = jnp.broadcast_to(run_ref[...], cnt_ref.shape)


def _mixer_out(l, x, attn, d, pool_w, pool_b, pool_scale, w_out, g1, n2, sh2, sc2, router_w, router_b):
    B, S, D = x.shape
    rows = min(OUT_ROWS, S)
    E = router_w.shape[-1]
    tile = lambda w: pl.BlockSpec((1, rows, w), lambda b, j: (b, j, 0))
    per_b = pl.BlockSpec((1, 1, D), lambda b, j: (b, 0, 0))
    lay = lambda a: _layer_spec(a, l)
    return pl.pallas_call(
        _mixer_out_kernel,
        grid=(B, S // rows),
        in_specs=[
            tile(D), tile(ATTN_WIDTH), tile(POOL_WIDTH),
            lay(pool_w), lay(pool_b), lay(pool_scale), lay(w_out),
            per_b, lay(n2), per_b, per_b, lay(router_w), lay(router_b),
        ],
        out_specs=[tile(D), pl.BlockSpec((rows * TOKEN_TILE, LANES), lambda b, j: (b * (S // rows) + j, 0)),
                   tile(LANES), pl.BlockSpec((8, E), lambda b, j: (0, 0))],
        out_shape=[
            jax.ShapeDtypeStruct((B, S, D), F32),
            jax.ShapeDtypeStruct((B * S * TOKEN_TILE, LANES), F32),
            jax.ShapeDtypeStruct((B, S, LANES), F32),
            jax.ShapeDtypeStruct((8, E), F32),
        ],
        scratch_shapes=[pltpu.VMEM((1, E), F32)],
        compiler_params=_cparams(("arbitrary", "arbitrary")),
        name="mixer_out",
    )(x, attn, d, pool_w, pool_b, pool_scale, w_out, g1, n2, sh2, sc2, router_w, router_b)


def _experts_kernel(be_ref, first_ref, nused_ref, tok_cur_ref, tok_n1_ref, tok_n2_ref, dst_prev_ref, dst_cur_ref,
                    h_ref, w1_ref, b1_ref, w2_ref, b2_ref, y_ref,
                    x0_ref, x1_ref, x2_ref, o0_ref, o1_ref, o2_ref, w1s_ref, w2s_ref, gsem, ssem, zsem,
                    *, dump_start, n_dump):
    i = pl.program_id(0)
    nused = nused_ref[0]
    rows = MOE_BLOCK
    tr = TOKEN_TILE
    f = w2_ref.shape[2]

    def tile_in(tok, r, xbuf, sem):
        return pltpu.make_async_copy(h_ref.at[pl.ds(pl.multiple_of(tok * tr, tr), tr)], xbuf.at[pl.ds(r * tr, tr)], sem)

    def tile_out(obuf, r, dst, sem):
        return pltpu.make_async_copy(obuf.at[pl.ds(r * tr, tr)], y_ref.at[pl.ds(pl.multiple_of(dst * tr, tr), tr)], sem)

    def wait_in(xbuf, sem):
        pltpu.make_async_copy(h_ref.at[pl.ds(0, rows * tr)], xbuf, sem).wait()

    def wait_out(obuf, sem):
        pltpu.make_async_copy(obuf, y_ref.at[pl.ds(0, rows * tr)], sem).wait()

    xs = (x0_ref, x1_ref, x2_ref)
    os_ = (o0_ref, o1_ref, o2_ref)

    def gather_loop(tok_ref, xbuf, sem):
        def body(r, carry):
            tile_in(tok_ref[0, 0, r], r, xbuf, sem).start()
            return carry
        lax.fori_loop(0, rows, body, 0, unroll=8)

    def step(m):
        p1, p2 = (m + 2) % 3, (m + 1) % 3
        xcur, ocur, oprev = xs[m], os_[m], os_[p1]

        if m == 0:
            @pl.when(i == 0)
            def _():
                o0_ref[...] = jnp.zeros_like(o0_ref)
                o2_ref[...] = jnp.zeros_like(o2_ref)
                for c in range(n_dump):
                    pltpu.make_async_copy(o0_ref, y_ref.at[pl.ds((dump_start + c * rows) * tr, rows * tr)], zsem).start()
                for c in range(n_dump):
                    pltpu.make_async_copy(o0_ref, y_ref.at[pl.ds((dump_start + c * rows) * tr, rows * tr)], zsem).wait()
                gather_loop(tok_cur_ref, x0_ref, gsem.at[0])
                gather_loop(tok_n1_ref, x1_ref, gsem.at[1])

        wait_in(xcur, gsem.at[m])

        @pl.when(i >= 2)
        def _():
            wait_out(ocur, ssem.at[m])

        @pl.when(first_ref[i] == 1)
        def _():
            w1s_ref[...] = w1_ref[0, 0].astype(BF16)
            w2s_ref[...] = w2_ref[0, 0].astype(BF16)

        for r in range(rows):
            tile_out(oprev, r, dst_prev_ref[0, 0, r], ssem.at[p1]).start()
        for r in range(rows):
            tile_in(tok_n2_ref[0, 0, r], r, xs[p1], gsem.at[p1]).start()
        x = jnp.concatenate([xcur[pl.ds(s, rows, stride=tr), :] for s in range(tr)], axis=1).astype(BF16)
        gu = jnp.dot(x, w1s_ref[...], preferred_element_type=F32) + b1_ref[0, 0]
        glu = jnp.minimum(gu[:, :f], SWIGLU_LIMIT)
        lin = jnp.clip(gu[:, f:], -SWIGLU_LIMIT, SWIGLU_LIMIT)
        act = glu * jax.nn.sigmoid(SWIGLU_ALPHA * glu) * (lin + 1.0)
        o = jnp.dot(act.astype(BF16), w2s_ref[...], preferred_element_type=F32) + b2_ref[0, 0]
        for s in range(tr):
            ocur[pl.ds(s, rows, stride=tr), :] = o[:, s * LANES:(s + 1) * LANES]

        @pl.when(i == nused - 1)
        def _():
            def last_scatter(r, carry):
                tile_out(ocur, r, dst_cur_ref[0, 0, r], ssem.at[m]).start()
                return carry
            lax.fori_loop(0, rows, last_scatter, 0, unroll=8)
            wait_out(os_[p2], ssem.at[p2])
            wait_out(oprev, ssem.at[p1])
            wait_out(ocur, ssem.at[m])
            wait_in(xs[p2], gsem.at[p2])
            wait_in(xs[p1], gsem.at[p1])

    for m in range(3):
        @pl.when((i < nused) & (i % 3 == m))
        def _(m=m):
            step(m)


def _experts(l, h2t, w1, b1, w2, b2, block_e, first, nused, tok3, dst3, n_rows, dump_start, n_dump):
    D, F2 = w1.shape[2:]
    F = w2.shape[2]
    nb = tok3.shape[0]
    idx_blk = (1, 1, MOE_BLOCK)
    smem = lambda fn: pl.BlockSpec(idx_blk, fn, memory_space=pltpu.SMEM)
    by_e = lambda i, be, fi, nu: (l, be[i], 0, 0)
    block_rows = MOE_BLOCK * TOKEN_TILE
    grid_spec = pltpu.PrefetchScalarGridSpec(
        num_scalar_prefetch=3,
        grid=(nb,),
        in_specs=[
            smem(lambda i, be, fi, nu: (i, 0, 0)),
            smem(lambda i, be, fi, nu: (jnp.minimum(i + 1, nb - 1), 0, 0)),
            smem(lambda i, be, fi, nu: (jnp.minimum(i + 2, nb - 1), 0, 0)),
            smem(lambda i, be, fi, nu: (jnp.where(i == 0, nb, i - 1), 0, 0)),
            smem(lambda i, be, fi, nu: (i, 0, 0)),
            pl.BlockSpec(memory_space=pl.ANY),
            pl.BlockSpec((1, 1, D, F2), by_e),
            pl.BlockSpec((1, 1, 1, F2), by_e),
            pl.BlockSpec((1, 1, F, D), by_e),
            pl.BlockSpec((1, 1, 1, D), by_e),
        ],
        out_specs=pl.BlockSpec(memory_space=pl.ANY),
        scratch_shapes=[pltpu.VMEM((block_rows, LANES), F32)] * 6 + [
            pltpu.VMEM((D, F2), BF16), pltpu.VMEM((F, D), BF16),
            pltpu.SemaphoreType.DMA((3,)), pltpu.SemaphoreType.DMA((3,)), pltpu.SemaphoreType.DMA(())],
    )
    return pl.pallas_call(
        functools.partial(_experts_kernel, dump_start=dump_start, n_dump=n_dump),
        grid_spec=grid_spec,
        out_shape=jax.ShapeDtypeStruct((n_rows * TOKEN_TILE, LANES), F32),
        compiler_params=_cparams(("arbitrary",)),
        name="experts",
    )(block_e, first, nused, tok3, tok3, tok3, dst3, dst3, h2t, w1, b1, w2, b2)


def _combine_kernel(y0_ref, y1_ref, y2_ref, y3_ref, route_ref, x_ref, g2_ref, o_ref):
    rows = x_ref.shape[0]
    route = route_ref[...]
    y = jnp.zeros(x_ref.shape, F32)
    for kk, yk_ref in enumerate((y0_ref, y1_ref, y2_ref, y3_ref)):
        yk = jnp.concatenate([yk_ref[pl.ds(s, rows, stride=TOKEN_TILE), :] for s in range(TOKEN_TILE)], axis=1)
        y = y + route[:, TOP_K + kk:TOP_K + kk + 1] * yk
    o_ref[...] = x_ref[...] + g2_ref[0] * y


def _combine(ybuf, route, x1, g2, seq):
    N, D = x1.shape
    rows = min(TOK_ROWS, N)
    steps = N // rows
    per_seq = seq // rows
    y_spec = lambda kk: pl.BlockSpec((rows * TOKEN_TILE, LANES), lambda i: (kk * steps + i, 0))
    return pl.pallas_call(
        _combine_kernel,
        grid=(steps,),
        in_specs=[y_spec(kk) for kk in range(TOP_K)] + [
            pl.BlockSpec((rows, LANES), lambda i: (i, 0)),
            pl.BlockSpec((rows, D), lambda i: (i, 0)),
            pl.BlockSpec((1, 1, D), lambda i: (i // per_seq, 0, 0)),
        ],
        out_specs=pl.BlockSpec((rows, D), lambda i: (i, 0)),
        out_shape=jax.ShapeDtypeStruct((N, D), F32),
        compiler_params=_cparams(("arbitrary",)),
        name="combine",
    )(ybuf, ybuf, ybuf, ybuf, route, x1, g2)


def _slots(route, counts, n_tokens):
    idx = route[:, :TOP_K].astype(jnp.int32)
    rank = route[:, 2 * TOP_K:3 * TOP_K].astype(jnp.int32)
    counts = counts.astype(jnp.int32)
    padded = (counts + MOE_BLOCK - 1) // MOE_BLOCK * MOE_BLOCK
    pend = jnp.cumsum(padded)
    pstart = pend - padded
    experts = jnp.arange(N_EXPERTS, dtype=jnp.int32)
    dest = rank + jnp.sum(jnp.where(idx[..., None] == experts, pstart, 0), axis=-1)
    n_pairs = n_tokens * TOP_K
    n_pad = N_EXPERTS * MOE_BLOCK
    n_blocks = (n_pairs + n_pad) // MOE_BLOCK
    nused = (pend[-1] // MOE_BLOCK).astype(jnp.int32)
    blk = jnp.minimum(jnp.arange(n_blocks, dtype=jnp.int32), nused - 1)
    block_e = jnp.sum((blk[:, None] * MOE_BLOCK >= pend[None, :]).astype(jnp.int32), axis=-1)
    first = jnp.concatenate([jnp.ones((1,), jnp.int32), (block_e[1:] != block_e[:-1]).astype(jnp.int32)])

    j = jnp.arange(MOE_BLOCK, dtype=jnp.int32)
    pad_key = jnp.where(j[None, :] < (padded - counts)[:, None], (pstart + counts)[:, None] + j[None, :], jnp.int32(2 ** 30))
    keys = jnp.concatenate([dest.reshape(-1), pad_key.reshape(-1)])
    ids = jnp.arange(n_pairs + n_pad, dtype=jnp.int32)
    _, src = lax.sort((keys, ids), num_keys=1)
    real = src < n_pairs
    tok = jnp.where(real, src // TOP_K, 0)
    dst = jnp.where(real, (src % TOP_K) * n_tokens + src // TOP_K, src)
    spare = n_pairs + n_pad + j
    tok3 = tok.reshape(n_blocks, 1, MOE_BLOCK)
    dst3 = jnp.concatenate([dst, spare]).reshape(n_blocks + 1, 1, MOE_BLOCK)
    n_rows = n_pairs + n_pad + MOE_BLOCK
    return block_e, first, nused.reshape(1), tok3, dst3, n_rows


def kernel(x, c, norm1_g, w_mod, b_mod, w_in, q_norm_g, k_norm_g, rel_bias, pool_w, pool_b, pool_scale, w_out,
           norm2_g, router_w, router_b, exp_w1, exp_b1, exp_w2, exp_b2):
    B, S, D = x.shape
    L = w_mod.shape[0]
    N = B * S
    E = router_w.shape[-1]
    row3 = lambda a: a.reshape(L, 1, -1)
    mod = _modulation(c, w_mod, b_mod)
    w_in_b, w_out_b, pool_w_b, router_w_b = (a.astype(BF16) for a in (w_in, w_out, pool_w, router_w))
    qg = row3(jnp.tile(q_norm_g, (1, N_HEADS)))
    kg = row3(jnp.tile(k_norm_g, (1, N_HEADS)))
    n1, n2, pb, ps, rb = row3(norm1_g), row3(norm2_g), row3(pool_b), row3(pool_scale), row3(router_b)
    b1 = exp_b1.reshape(L, E, 1, -1)
    b2 = exp_b2.reshape(L, E, 1, -1)
    for l in range(L):
        sh1, sc1, g1, sh2, sc2, g2 = [mod[l, :, i * D:(i + 1) * D].reshape(B, 1, D) for i in range(6)]
        q, k, v, d = _mixer_in(l, x, sh1, sc1, n1, w_in_b, qg, kg)
        attn = _attention(q, k, v, _bias_diagonals(rel_bias[l]))
        x1, h2t, route, counts = _mixer_out(l, x, attn, d, pool_w_b, pb, ps, w_out_b, g1, n2, sh2, sc2, router_w_b, rb)
        route = route.reshape(N, LANES)
        block_e, first, nused, tok3, dst3, n_rows = _slots(route, counts[0], N)
        ybuf = _experts(l, h2t, exp_w1, b1, exp_w2, b2, block_e, first, nused, tok3, dst3, n_rows,
                        dump_start=N * TOP_K, n_dump=N_EXPERTS)
        x = _combine(ybuf, route, x1.reshape(N, D), g2, S).reshape(B, S, D)
    return x
```

```python
import functools

import jax
import jax.numpy as jnp
from jax import lax
from jax.experimental import pallas as pl
from jax.experimental.pallas import tpu as pltpu

F32 = jnp.float32
BF16 = jnp.bfloat16

CHUNK = 64
LEFT_CHUNKS = 8
BAND = LEFT_CHUNKS + 1
HEAD_DIM = 64
N_HEADS = 8
ATTN_WIDTH = N_HEADS * HEAD_DIM
MAX_REL = 2 * CHUNK
POOL_WINDOWS = (2, 4, 8, 16)
POOL_GROUP = 128
POOL_WIDTH = POOL_GROUP * len(POOL_WINDOWS)
N_EXPERTS = 32
TOP_K = 4
SWIGLU_ALPHA = 1.702
SWIGLU_LIMIT = 7.0
MOE_BLOCK = 256
NORM_EPS = 1e-6
MASK_VALUE = -1e30

LANES = 128
TOKEN_TILE = 8
POOL_HALO = 16

IN_ROWS = 1024
ATTN_CHUNKS = 4
OUT_ROWS = 1024
TOK_ROWS = 512
VMEM_LIMIT = 56 * 1024 * 1024


def _cparams(sem, vmem=VMEM_LIMIT):
    return pltpu.CompilerParams(dimension_semantics=sem, vmem_limit_bytes=vmem)


def _layer_spec(a, l):
    zeros = (0,) * (a.ndim - 1)
    return pl.BlockSpec((1,) + a.shape[1:], lambda *_: (l,) + zeros)


def _mod_kernel(c_ref, w_ref, b_ref, o_ref):
    c = c_ref[...]
    cond = c * jax.nn.sigmoid(c)
    o_ref[0] = jnp.dot(cond.astype(BF16), w_ref[0].astype(BF16), preferred_element_type=F32) + b_ref[0]


def _modulation(c, w_mod, b_mod):
    L, D, W = w_mod.shape
    B = c.shape[0]
    tn = 1024
    return pl.pallas_call(
        _mod_kernel,
        grid=(L, W // tn),
        in_specs=[
            pl.BlockSpec((B, D), lambda l, j: (0, 0)),
            pl.BlockSpec((1, D, tn), lambda l, j: (l, 0, j)),
            pl.BlockSpec((1, 1, tn), lambda l, j: (l, 0, j)),
        ],
        out_specs=pl.BlockSpec((1, B, tn), lambda l, j: (l, 0, j)),
        out_shape=jax.ShapeDtypeStruct((L, B, W), F32),
        compiler_params=_cparams(("arbitrary", "arbitrary")),
        name="modulation",
    )(c, w_mod, b_mod.reshape(L, 1, W))


def _head_norm(t, gain):
    rows = t.shape[0]
    lane = lax.broadcasted_iota(jnp.int32, (rows, LANES), 1)
    first = lane < HEAD_DIM
    outs = []
    for p in range(ATTN_WIDTH // LANES):
        blk = t[:, p * LANES:(p + 1) * LANES]
        sq = blk * blk
        sa = jnp.sum(jnp.where(first, sq, 0.0), axis=-1, keepdims=True)
        sb = jnp.sum(jnp.where(first, 0.0, sq), axis=-1, keepdims=True)
        ra = lax.rsqrt(sa * (1.0 / HEAD_DIM) + NORM_EPS)
        rb = lax.rsqrt(sb * (1.0 / HEAD_DIM) + NORM_EPS)
        outs.append(blk * jnp.where(first, ra, rb))
    return jnp.concatenate(outs, axis=-1) * gain


def _mixer_in_kernel(x_ref, sh_ref, sc_ref, g_ref, w_ref, qg_ref, kg_ref,
                     q_ref, k_ref, v_ref, d_ref, halo_ref):
    j = pl.program_id(1)
    rows = x_ref.shape[1]
    x = x_ref[0]
    ms = jnp.mean(x * x, axis=-1, keepdims=True)
    h = (x * lax.rsqrt(ms + NORM_EPS) * g_ref[0]) * (1.0 + sc_ref[0]) + sh_ref[0]
    z = jnp.dot(h.astype(BF16), w_ref[0], preferred_element_type=F32)
    aw = ATTN_WIDTH
    q_ref[0] = (_head_norm(z[:, :aw], qg_ref[0]) * (HEAD_DIM ** -0.5)).astype(q_ref.dtype)
    k_ref[0] = _head_norm(z[:, aw:2 * aw], kg_ref[0]).astype(k_ref.dtype)
    v_ref[0] = z[:, 2 * aw:3 * aw].astype(v_ref.dtype)

    u = z[:, 3 * aw:]

    @pl.when(j == 0)
    def _():
        halo_ref[...] = jnp.zeros_like(halo_ref)

    t_idx = j * rows + lax.broadcasted_iota(jnp.int32, (rows, 1), 0)
    for gi, w in enumerate(POOL_WINDOWS):
        ls = slice(gi * POOL_GROUP, (gi + 1) * POOL_GROUP)
        ug = u[:, ls]
        ext = jnp.concatenate([halo_ref[:, ls], ug], axis=0)
        acc = ext
        span = 1
        while span < w:
            acc = acc + pltpu.roll(acc, span, 0)
            span *= 2
        win = acc[POOL_HALO:, :]
        count = jnp.minimum(t_idx + 1, w).astype(F32)
        d_ref[0, :, ls] = (win / count - ug).astype(d_ref.dtype)
    halo_ref[...] = u[rows - POOL_HALO:, :]


def _mixer_in(l, x, sh, sc, g, w_in, qg, kg):
    B, S, D = x.shape
    rows = min(IN_ROWS, S)
    o_spec = pl.BlockSpec((1, rows, ATTN_WIDTH), lambda b, j: (b, j, 0))
    o_shape = jax.ShapeDtypeStruct((B, S, ATTN_WIDTH), BF16)
    per_b = pl.BlockSpec((1, 1, D), lambda b, j: (b, 0, 0))
    return pl.pallas_call(
        _mixer_in_kernel,
        grid=(B, S // rows),
        in_specs=[
            pl.BlockSpec((1, rows, D), lambda b, j: (b, j, 0)),
            per_b, per_b, _layer_spec(g, l), _layer_spec(w_in, l), _layer_spec(qg, l), _layer_spec(kg, l),
        ],
        out_specs=[o_spec, o_spec, o_spec, o_spec],
        out_shape=[o_shape, o_shape, o_shape, o_shape],
        scratch_shapes=[pltpu.VMEM((POOL_HALO, POOL_WIDTH), F32)],
        compiler_params=_cparams(("arbitrary", "arbitrary")),
        name="mixer_in",
    )(x, sh, sc, g, w_in, qg, kg)


def _attn_kernel(q_ref, k0_ref, k1_ref, k2_ref, v0_ref, v1_ref, v2_ref, diag_ref, o_ref, bias_ref):
    i = pl.program_id(1)
    tq = q_ref.shape[1]
    k_refs = (k0_ref, k1_ref, k2_ref)
    v_refs = (v0_ref, v1_ref, v2_ref)
    nt = len(k_refs)

    @pl.when((pl.program_id(0) == 0) & (i == 0))
    def _():
        row = lax.broadcasted_iota(jnp.int32, (tq, nt * tq), 0)
        col = lax.broadcasted_iota(jnp.int32, (tq, nt * tq), 1)
        in_band = col - (row // CHUNK) * CHUNK
        ok = (in_band >= 0) & (in_band < BAND * CHUNK)
        width = diag_ref.shape[1]
        for h in range(N_HEADS):
            table = jnp.broadcast_to(diag_ref[h:h + 1, :], (tq, width))
            toeplitz = pltpu.roll(table, width - tq, 1, stride=1, stride_axis=0)[:, :nt * tq]
            bias_ref[h // 2, (h % 2) * tq:(h % 2 + 1) * tq, :] = jnp.where(ok, toeplitz, MASK_VALUE)

    lane = lax.broadcasted_iota(jnp.int32, (tq, LANES), 1)
    first = lane < HEAD_DIM
    nt_dims = (((1,), (1,)), ((), ()))

    kpos = lax.broadcasted_iota(jnp.int32, (2 * tq, nt * tq), 1)
    in_seq = kpos >= (nt - 1 - i) * tq
    for p in range(ATTN_WIDTH // LANES):
        ls = slice(p * LANES, (p + 1) * LANES)
        q = q_ref[0, :, ls]
        zero = jnp.zeros_like(q)
        q2 = jnp.concatenate([jnp.where(first, q, zero), jnp.where(first, zero, q)], axis=0)
        s = jnp.concatenate(
            [lax.dot_general(q2, kr[0, :, ls], nt_dims, preferred_element_type=F32) for kr in k_refs], axis=1)
        s = jnp.where(in_seq, s + bias_ref[p], MASK_VALUE)
        m = jnp.max(s, axis=-1, keepdims=True)
        e = jnp.exp(s - m)
        l = jnp.sum(e, axis=-1, keepdims=True)
        eb = e.astype(BF16)
        pv = jnp.dot(eb[:, :tq], v_refs[0][0, :, ls], preferred_element_type=F32)
        for n in range(1, nt):
            pv = pv + jnp.dot(eb[:, n * tq:(n + 1) * tq], v_refs[n][0, :, ls], preferred_element_type=F32)
        pv = pv / l
        o_ref[0, :, ls] = jnp.where(first, pv[:tq], pv[tq:]).astype(o_ref.dtype)


def _attention(q, k, v, diag):
    B, S, W = q.shape
    tq = ATTN_CHUNKS * CHUNK
    assert LEFT_CHUNKS == 2 * ATTN_CHUNKS and S % tq == 0
    cur = pl.BlockSpec((1, tq, W), lambda b, i: (b, i, 0))
    back = lambda n: pl.BlockSpec((1, tq, W), lambda b, i: (b, jnp.maximum(i - n, 0), 0))
    return pl.pallas_call(
        _attn_kernel,
        grid=(B, S // tq),
        in_specs=[cur, back(2), back(1), cur, back(2), back(1), cur,
                  pl.BlockSpec(diag.shape, lambda b, i: (0, 0))],
        out_specs=cur,
        out_shape=jax.ShapeDtypeStruct((B, S, W), BF16),
        scratch_shapes=[pltpu.VMEM((N_HEADS // 2, 2 * tq, 3 * tq), F32)],
        compiler_params=_cparams(("arbitrary", "arbitrary")),
        name="attention",
    )(q, k, k, k, v, v, v, diag)


def _bias_diagonals(rel_bias):
    tq = ATTN_CHUNKS * CHUNK
    H = rel_bias.shape[0]
    lo = tq + LEFT_CHUNKS * CHUNK - MAX_REL
    hi = 4 * tq - lo - (2 * MAX_REL + 1)
    return jnp.concatenate([jnp.broadcast_to(rel_bias[:, :1], (H, lo)), rel_bias,
                            jnp.broadcast_to(rel_bias[:, -1:], (H, hi))], axis=1).astype(F32)


def _mixer_out_kernel(x_ref, a_ref, d_ref, pw_ref, pb_ref, ps_ref, wo_ref, g1_ref, n2_ref, sh_ref, sc_ref,
                      rw_ref, rb_ref, x1_ref, h2_ref, route_ref, cnt_ref, run_ref):
    step = pl.program_id(0) * pl.num_programs(1) + pl.program_id(1)
    rows = x_ref.shape[1]

    @pl.when(step == 0)
    def _():
        run_ref[...] = jnp.zeros_like(run_ref)

    y = jnp.dot(a_ref[0], wo_ref[0, :ATTN_WIDTH, :], preferred_element_type=F32)
    for gi in range(len(POOL_WINDOWS)):
        ls = slice(gi * POOL_GROUP, (gi + 1) * POOL_GROUP)
        pg = jnp.dot(d_ref[0, :, ls], pw_ref[0, gi], preferred_element_type=F32)
        pg = (pg + pb_ref[0, :, ls]) * ps_ref[0, :, ls]
        lo = ATTN_WIDTH + gi * POOL_GROUP
        y = y + jnp.dot(pg.astype(BF16), wo_ref[0, lo:lo + POOL_GROUP, :], preferred_element_type=F32)
    x1 = x_ref[0] + g1_ref[0] * y
    x1_ref[0] = x1

    ms = jnp.mean(x1 * x1, axis=-1, keepdims=True)
    h2 = (x1 * lax.rsqrt(ms + NORM_EPS) * n2_ref[0]) * (1.0 + sc_ref[0]) + sh_ref[0]
    for s in range(TOKEN_TILE):
        h2_ref[pl.ds(s, rows, stride=TOKEN_TILE), :] = h2[:, s * LANES:(s + 1) * LANES]

    logits = jnp.dot(h2.astype(BF16), rw_ref[0], preferred_element_type=F32) + rb_ref[0]
    ne = logits.shape[-1]
    elane = lax.broadcasted_iota(jnp.int32, (rows, ne), 1)
    work = logits
    picked = jnp.zeros((rows, ne), F32)
    vals, idxs = [], []
    for _ in range(TOP_K):
        m = jnp.max(work, axis=-1, keepdims=True)
        idx = jnp.min(jnp.where(work == m, elane, ne), axis=-1, keepdims=True)
        sel = elane == idx
        vals.append(m)
        idxs.append(idx)
        work = jnp.where(sel, -jnp.inf, work)
        picked = picked + sel.astype(F32)
    exps = [jnp.exp(v - vals[0]) for v in vals]
    denom = exps[0] + exps[1] + exps[2] + exps[3]

    r_i = lax.broadcasted_iota(jnp.int32, (rows, rows), 0)
    c_i = lax.broadcasted_iota(jnp.int32, (rows, rows), 1)
    tri = (c_i < r_i).astype(BF16)
    before = jnp.dot(tri, picked.astype(BF16), preferred_element_type=F32) + run_ref[...]
    run_ref[...] = run_ref[...] + jnp.sum(picked, axis=0, keepdims=True)

    lane = lax.broadcasted_iota(jnp.int32, (rows, LANES), 1)
    route = jnp.zeros((rows, LANES), F32)
    for kk in range(TOP_K):
        rank = jnp.sum(jnp.where(elane == idxs[kk], before, 0.0), axis=-1, keepdims=True)
        route = jnp.where(lane == kk, idxs[kk].astype(F32), route)
        route = jnp.where(lane == TOP_K + kk, exps[kk] / denom, route)
        route = jnp.where(lane == 2 * TOP_K + kk, rank, route)
    route_ref[0] = route
    cnt_ref[...] = jnp.broadcast_to(run_ref[...], cnt_ref.shape)


def _mixer_out(l, x, attn, d, pool_w, pool_b, pool_scale, w_out, g1, n2, sh2, sc2, router_w, router_b):
    B, S, D = x.shape
    rows = min(OUT_ROWS, S)
    E = router_w.shape[-1]
    tile = lambda w: pl.BlockSpec((1, rows, w), lambda b, j: (b, j, 0))
    per_b = pl.BlockSpec((1, 1, D), lambda b, j: (b, 0, 0))
    lay = lambda a: _layer_spec(a, l)
    return pl.pallas_call(
        _mixer_out_kernel,
        grid=(B, S // rows),
        in_specs=[
            tile(D), tile(ATTN_WIDTH), tile(POOL_WIDTH),
            lay(pool_w), lay(pool_b), lay(pool_scale), lay(w_out),
            per_b, lay(n2), per_b, per_b, lay(router_w), lay(router_b),
        ],
        out_specs=[tile(D), pl.BlockSpec((rows * TOKEN_TILE, LANES), lambda b, j: (b * (S // rows) + j, 0)),
                   tile(LANES), pl.BlockSpec((8, E), lambda b, j: (0, 0))],
        out_shape=[
            jax.ShapeDtypeStruct((B, S, D), F32),
            jax.ShapeDtypeStruct((B * S * TOKEN_TILE, LANES), F32),
            jax.ShapeDtypeStruct((B, S, LANES), F32),
            jax.ShapeDtypeStruct((8, E), F32),
        ],
        scratch_shapes=[pltpu.VMEM((1, E), F32)],
        compiler_params=_cparams(("arbitrary", "arbitrary")),
        name="mixer_out",
    )(x, attn, d, pool_w, pool_b, pool_scale, w_out, g1, n2, sh2, sc2, router_w, router_b)


def _experts_kernel(be_ref, first_ref, nused_ref, wslot_ref, next_e_ref,
                    tok_cur_ref, tok_n1_ref, tok_n2_ref, dst_prev_ref, dst_cur_ref,
                    h_ref, w1_ref, b1_ref, w2_ref, b2_ref, y_ref,
                    x0_ref, x1_ref, x2_ref, o0_ref, o1_ref, o2_ref, w1f_ref, w2f_ref, w1s_ref, w2s_ref,
                    gsem, ssem, zsem, wsem, *, layer, dump_start, n_dump):
    i = pl.program_id(0)
    nused = nused_ref[0]
    rows = MOE_BLOCK
    tr = TOKEN_TILE
    f = w2_ref.shape[2]

    def tile_in(tok, r, xbuf, sem):
        return pltpu.make_async_copy(h_ref.at[pl.ds(pl.multiple_of(tok * tr, tr), tr)], xbuf.at[pl.ds(r * tr, tr)], sem)

    def tile_out(obuf, r, dst, sem):
        return pltpu.make_async_copy(obuf.at[pl.ds(r * tr, tr)], y_ref.at[pl.ds(pl.multiple_of(dst * tr, tr), tr)], sem)

    def wait_in(xbuf, sem):
        pltpu.make_async_copy(h_ref.at[pl.ds(0, rows * tr)], xbuf, sem).wait()

    def wait_out(obuf, sem):
        pltpu.make_async_copy(obuf, y_ref.at[pl.ds(0, rows * tr)], sem).wait()

    xs = (x0_ref, x1_ref, x2_ref)
    os_ = (o0_ref, o1_ref, o2_ref)

    def gather_loop(tok_ref, xbuf, sem):
        def body(r, carry):
            tile_in(tok_ref[0, 0, r], r, xbuf, sem).start()
            return carry
        lax.fori_loop(0, rows, body, 0, unroll=8)

    def step(m):
        p1, p2 = (m + 2) % 3, (m + 1) % 3
        xcur, ocur, oprev = xs[m], os_[m], os_[p1]

        if m == 0:
            @pl.when(i == 0)
            def _():
                o0_ref[...] = jnp.zeros_like(o0_ref)
                o2_ref[...] = jnp.zeros_like(o2_ref)
                for c in range(n_dump):
                    pltpu.make_async_copy(o0_ref, y_ref.at[pl.ds((dump_start + c * rows) * tr, rows * tr)], zsem).start()
                for c in range(n_dump):
                    pltpu.make_async_copy(o0_ref, y_ref.at[pl.ds((dump_start + c * rows) * tr, rows * tr)], zsem).wait()
                gather_loop(tok_cur_ref, x0_ref, gsem.at[0])
                gather_loop(tok_n1_ref, x1_ref, gsem.at[1])

        wait_in(xcur, gsem.at[m])

        @pl.when(i >= 2)
        def _():
            wait_out(ocur, ssem.at[m])

        @pl.when(first_ref[i] == 1)
        def _():
            slot = wslot_ref[i]

            def weights_in(e, sl):
                return (pltpu.make_async_copy(w1_ref.at[layer, e], w1f_ref.at[sl], wsem.at[0, sl]),
                        pltpu.make_async_copy(w2_ref.at[layer, e], w2f_ref.at[sl], wsem.at[1, sl]))

            @pl.when(i == 0)
            def _():
                for cp in weights_in(be_ref[0], 0):
                    cp.start()

            for cp in weights_in(be_ref[i], slot):
                cp.wait()

            @pl.when(next_e_ref[i] < N_EXPERTS)
            def _():
                for cp in weights_in(next_e_ref[i], 1 - slot):
                    cp.start()

            w1s_ref[...] = w1f_ref[slot].astype(BF16)
            w2s_ref[...] = w2f_ref[slot].astype(BF16)

        for r in range(rows):
            tile_out(oprev, r, dst_prev_ref[0, 0, r], ssem.at[p1]).start()
        for r in range(rows):
            tile_in(tok_n2_ref[0, 0, r], r, xs[p1], gsem.at[p1]).start()
        x = jnp.concatenate([xcur[pl.ds(s, rows, stride=tr), :] for s in range(tr)], axis=1).astype(BF16)
        gu = jnp.dot(x, w1s_ref[...], preferred_element_type=F32) + b1_ref[0, 0]
        glu = jnp.minimum(gu[:, :f], SWIGLU_LIMIT)
        lin = jnp.clip(gu[:, f:], -SWIGLU_LIMIT, SWIGLU_LIMIT)
        act = glu * jax.nn.sigmoid(SWIGLU_ALPHA * glu) * (lin + 1.0)
        o = jnp.dot(act.astype(BF16), w2s_ref[...], preferred_element_type=F32) + b2_ref[0, 0]
        for s in range(tr):
            ocur[pl.ds(s, rows, stride=tr), :] = o[:, s * LANES:(s + 1) * LANES]

        @pl.when(i == nused - 1)
        def _():
            def last_scatter(r, carry):
                tile_out(ocur, r, dst_cur_ref[0, 0, r], ssem.at[m]).start()
                return carry
            lax.fori_loop(0, rows, last_scatter, 0, unroll=8)
            wait_out(os_[p2], ssem.at[p2])
            wait_out(oprev, ssem.at[p1])
            wait_out(ocur, ssem.at[m])
            wait_in(xs[p2], gsem.at[p2])
            wait_in(xs[p1], gsem.at[p1])

    for m in range(3):
        @pl.when((i < nused) & (i % 3 == m))
        def _(m=m):
            step(m)


def _experts(l, h2t, w1, b1, w2, b2, block_e, first, nused, wslot, next_e, tok3, dst3, n_rows, dump_start, n_dump):
    D, F2 = w1.shape[2:]
    F = w2.shape[2]
    nb = tok3.shape[0]
    idx_blk = (1, 1, MOE_BLOCK)
    smem = lambda fn: pl.BlockSpec(idx_blk, fn, memory_space=pltpu.SMEM)
    by_e = lambda i, be, *_: (l, be[i], 0, 0)
    block_rows = MOE_BLOCK * TOKEN_TILE
    grid_spec = pltpu.PrefetchScalarGridSpec(
        num_scalar_prefetch=5,
        grid=(nb,),
        in_specs=[
            smem(lambda i, *_: (i, 0, 0)),
            smem(lambda i, *_: (jnp.minimum(i + 1, nb - 1), 0, 0)),
            smem(lambda i, *_: (jnp.minimum(i + 2, nb - 1), 0, 0)),
            smem(lambda i, *_: (jnp.where(i == 0, nb, i - 1), 0, 0)),
            smem(lambda i, *_: (i, 0, 0)),
            pl.BlockSpec(memory_space=pl.ANY),
            pl.BlockSpec(memory_space=pl.ANY),
            pl.BlockSpec((1, 1, 1, F2), by_e),
            pl.BlockSpec(memory_space=pl.ANY),
            pl.BlockSpec((1, 1, 1, D), by_e),
        ],
        out_specs=pl.BlockSpec(memory_space=pl.ANY),
        scratch_shapes=[pltpu.VMEM((block_rows, LANES), F32)] * 6 + [
            pltpu.VMEM((2, D, F2), F32), pltpu.VMEM((2, F, D), F32),
            pltpu.VMEM((D, F2), BF16), pltpu.VMEM((F, D), BF16),
            pltpu.SemaphoreType.DMA((3,)), pltpu.SemaphoreType.DMA((3,)), pltpu.SemaphoreType.DMA(()),
            pltpu.SemaphoreType.DMA((2, 2))],
    )
    return pl.pallas_call(
        functools.partial(_experts_kernel, layer=l, dump_start=dump_start, n_dump=n_dump),
        grid_spec=grid_spec,
        out_shape=jax.ShapeDtypeStruct((n_rows * TOKEN_TILE, LANES), F32),
        compiler_params=_cparams(("arbitrary",)),
        name="experts",
    )(block_e, first, nused, wslot, next_e, tok3, tok3, tok3, dst3, dst3, h2t, w1, b1, w2, b2)


def _combine_kernel(y0_ref, y1_ref, y2_ref, y3_ref, route_ref, x_ref, g2_ref, o_ref):
    rows = x_ref.shape[0]
    route = route_ref[...]
    y = jnp.zeros(x_ref.shape, F32)
    for kk, yk_ref in enumerate((y0_ref, y1_ref, y2_ref, y3_ref)):
        yk = jnp.concatenate([yk_ref[pl.ds(s, rows, stride=TOKEN_TILE), :] for s in range(TOKEN_TILE)], axis=1)
        y = y + route[:, TOP_K + kk:TOP_K + kk + 1] * yk
    o_ref[...] = x_ref[...] + g2_ref[0] * y


def _combine(ybuf, route, x1, g2, seq):
    N, D = x1.shape
    rows = min(TOK_ROWS, N)
    steps = N // rows
    per_seq = seq // rows
    y_spec = lambda kk: pl.BlockSpec((rows * TOKEN_TILE, LANES), lambda i: (kk * steps + i, 0))
    return pl.pallas_call(
        _combine_kernel,
        grid=(steps,),
        in_specs=[y_spec(kk) for kk in range(TOP_K)] + [
            pl.BlockSpec((rows, LANES), lambda i: (i, 0)),
            pl.BlockSpec((rows, D), lambda i: (i, 0)),
            pl.BlockSpec((1, 1, D), lambda i: (i // per_seq, 0, 0)),
        ],
        out_specs=pl.BlockSpec((rows, D), lambda i: (i, 0)),
        out_shape=jax.ShapeDtypeStruct((N, D), F32),
        compiler_params=_cparams(("arbitrary",)),
        name="combine",
    )(ybuf, ybuf, ybuf, ybuf, route, x1, g2)


def _slots(route, counts, n_tokens):
    idx = route[:, :TOP_K].astype(jnp.int32)
    rank = route[:, 2 * TOP_K:3 * TOP_K].astype(jnp.int32)
    counts = counts.astype(jnp.int32)
    padded = (counts + MOE_BLOCK - 1) // MOE_BLOCK * MOE_BLOCK
    pend = jnp.cumsum(padded)
    pstart = pend - padded
    experts = jnp.arange(N_EXPERTS, dtype=jnp.int32)
    dest = rank + jnp.sum(jnp.where(idx[..., None] == experts, pstart, 0), axis=-1)
    n_pairs = n_tokens * TOP_K
    n_pad = N_EXPERTS * MOE_BLOCK
    n_blocks = (n_pairs + n_pad) // MOE_BLOCK
    nused = (pend[-1] // MOE_BLOCK).astype(jnp.int32)
    blk = jnp.minimum(jnp.arange(n_blocks, dtype=jnp.int32), nused - 1)
    block_e = jnp.sum((blk[:, None] * MOE_BLOCK >= pend[None, :]).astype(jnp.int32), axis=-1)
    first = jnp.concatenate([jnp.ones((1,), jnp.int32), (block_e[1:] != block_e[:-1]).astype(jnp.int32)])
    wslot = (jnp.cumsum(first) - 1) % 2
    later = (experts[None, :] > block_e[:, None]) & (padded > 0)[None, :]
    next_e = jnp.min(jnp.where(later, experts[None, :], N_EXPERTS), axis=-1).astype(jnp.int32)

    j = jnp.arange(MOE_BLOCK, dtype=jnp.int32)
    pad_key = jnp.where(j[None, :] < (padded - counts)[:, None], (pstart + counts)[:, None] + j[None, :], jnp.int32(2 ** 30))
    keys = jnp.concatenate([dest.reshape(-1), pad_key.reshape(-1)])
    ids = jnp.arange(n_pairs + n_pad, dtype=jnp.int32)
    _, src = lax.sort((keys, ids), num_keys=1)
    real = src < n_pairs
    tok = jnp.where(real, src // TOP_K, 0)
    dst = jnp.where(real, (src % TOP_K) * n_tokens + src // TOP_K, src)
    spare = n_pairs + n_pad + j
    tok3 = tok.reshape(n_blocks, 1, MOE_BLOCK)
    dst3 = jnp.concatenate([dst, spare]).reshape(n_blocks + 1, 1, MOE_BLOCK)
    n_rows = n_pairs + n_pad + MOE_BLOCK
    return block_e, first, nused.reshape(1), wslot.astype(jnp.int32), next_e, tok3, dst3, n_rows


def kernel(x, c, norm1_g, w_mod, b_mod, w_in, q_norm_g, k_norm_g, rel_bias, pool_w, pool_b, pool_scale, w_out,
           norm2_g, router_w, router_b, exp_w1, exp_b1, exp_w2, exp_b2):
    B, S, D = x.shape
    L = w_mod.shape[0]
    N = B * S
    E = router_w.shape[-1]
    row3 = lambda a: a.reshape(L, 1, -1)
    mod = _modulation(c, w_mod, b_mod)
    w_in_b, w_out_b, pool_w_b, router_w_b = (a.astype(BF16) for a in (w_in, w_out, pool_w, router_w))
    qg = row3(jnp.tile(q_norm_g, (1, N_HEADS)))
    kg = row3(jnp.tile(k_norm_g, (1, N_HEADS)))
    n1, n2, pb, ps, rb = row3(norm1_g), row3(norm2_g), row3(pool_b), row3(pool_scale), row3(router_b)
    b1 = exp_b1.reshape(L, E, 1, -1)
    b2 = exp_b2.reshape(L, E, 1, -1)
    for l in range(L):
        sh1, sc1, g1, sh2, sc2, g2 = [mod[l, :, i * D:(i + 1) * D].reshape(B, 1, D) for i in range(6)]
        q, k, v, d = _mixer_in(l, x, sh1, sc1, n1, w_in_b, qg, kg)
        attn = _attention(q, k, v, _bias_diagonals(rel_bias[l]))
        x1, h2t, route, counts = _mixer_out(l, x, attn, d, pool_w_b, pb, ps, w_out_b, g1, n2, sh2, sc2, router_w_b, rb)
        route = route.reshape(N, LANES)
        block_e, first, nused, wslot, next_e, tok3, dst3, n_rows = _slots(route, counts[0], N)
        ybuf = _experts(l, h2t, exp_w1, b1, exp_w2, b2, block_e, first, nused, wslot, next_e, tok3, dst3, n_rows,
                        dump_start=N * TOP_K, n_dump=N_EXPERTS)
        x = _combine(ybuf, route, x1.reshape(N, D), g2, S).reshape(B, S, D)
    return x
```

```python
import functools

import jax
import jax.numpy as jnp
from jax import lax
from jax.experimental import pallas as pl
from jax.experimental.pallas import tpu as pltpu

F32 = jnp.float32
BF16 = jnp.bfloat16

CHUNK = 64
LEFT_CHUNKS = 8
BAND = LEFT_CHUNKS + 1
HEAD_DIM = 64
N_HEADS = 8
ATTN_WIDTH = N_HEADS * HEAD_DIM
MAX_REL = 2 * CHUNK
POOL_WINDOWS = (2, 4, 8, 16)
POOL_GROUP = 128
POOL_WIDTH = POOL_GROUP * len(POOL_WINDOWS)
N_EXPERTS = 32
TOP_K = 4
SWIGLU_ALPHA = 1.702
SWIGLU_LIMIT = 7.0
MOE_BLOCK = 256
NORM_EPS = 1e-6
MASK_VALUE = -1e30

LANES = 128
TOKEN_TILE = 8
POOL_HALO = 16

IN_ROWS = 1024
ATTN_CHUNKS = 4
OUT_ROWS = 1024
TOK_ROWS = 512
VMEM_LIMIT = 56 * 1024 * 1024


def _cparams(sem, vmem=VMEM_LIMIT):
    return pltpu.CompilerParams(dimension_semantics=sem, vmem_limit_bytes=vmem)


def _layer_spec(a, l):
    zeros = (0,) * (a.ndim - 1)
    return pl.BlockSpec((1,) + a.shape[1:], lambda *_: (l,) + zeros)


def _mod_kernel(c_ref, w_ref, b_ref, o_ref):
    c = c_ref[...]
    cond = c * jax.nn.sigmoid(c)
    o_ref[0] = jnp.dot(cond.astype(BF16), w_ref[0].astype(BF16), preferred_element_type=F32) + b_ref[0]


def _modulation(c, w_mod, b_mod):
    L, D, W = w_mod.shape
    B = c.shape[0]
    tn = 1024
    return pl.pallas_call(
        _mod_kernel,
        grid=(L, W // tn),
        in_specs=[
            pl.BlockSpec((B, D), lambda l, j: (0, 0)),
            pl.BlockSpec((1, D, tn), lambda l, j: (l, 0, j)),
            pl.BlockSpec((1, 1, tn), lambda l, j: (l, 0, j)),
        ],
        out_specs=pl.BlockSpec((1, B, tn), lambda l, j: (l, 0, j)),
        out_shape=jax.ShapeDtypeStruct((L, B, W), F32),
        compiler_params=_cparams(("arbitrary", "arbitrary")),
        name="modulation",
    )(c, w_mod, b_mod.reshape(L, 1, W))


def _head_norm(t, gain):
    rows = t.shape[0]
    lane = lax.broadcasted_iota(jnp.int32, (rows, LANES), 1)
    first = lane < HEAD_DIM
    outs = []
    for p in range(ATTN_WIDTH // LANES):
        blk = t[:, p * LANES:(p + 1) * LANES]
        sq = blk * blk
        sa = jnp.sum(jnp.where(first, sq, 0.0), axis=-1, keepdims=True)
        sb = jnp.sum(jnp.where(first, 0.0, sq), axis=-1, keepdims=True)
        ra = lax.rsqrt(sa * (1.0 / HEAD_DIM) + NORM_EPS)
        rb = lax.rsqrt(sb * (1.0 / HEAD_DIM) + NORM_EPS)
        outs.append(blk * jnp.where(first, ra, rb))
    return jnp.concatenate(outs, axis=-1) * gain


def _mixer_in_kernel(x_ref, sh_ref, sc_ref, g_ref, w_ref, qg_ref, kg_ref,
                     q_ref, k_ref, v_ref, d_ref, halo_ref):
    j = pl.program_id(1)
    rows = x_ref.shape[1]
    x = x_ref[0]
    ms = jnp.mean(x * x, axis=-1, keepdims=True)
    h = (x * lax.rsqrt(ms + NORM_EPS) * g_ref[0]) * (1.0 + sc_ref[0]) + sh_ref[0]
    z = jnp.dot(h.astype(BF16), w_ref[0], preferred_element_type=F32)
    aw = ATTN_WIDTH
    q_ref[0] = (_head_norm(z[:, :aw], qg_ref[0]) * (HEAD_DIM ** -0.5)).astype(q_ref.dtype)
    k_ref[0] = _head_norm(z[:, aw:2 * aw], kg_ref[0]).astype(k_ref.dtype)
    v_ref[0] = z[:, 2 * aw:3 * aw].astype(v_ref.dtype)

    u = z[:, 3 * aw:]

    @pl.when(j == 0)
    def _():
        halo_ref[...] = jnp.zeros_like(halo_ref)

    t_idx = j * rows + lax.broadcasted_iota(jnp.int32, (rows, 1), 0)
    for gi, w in enumerate(POOL_WINDOWS):
        ls = slice(gi * POOL_GROUP, (gi + 1) * POOL_GROUP)
        ug = u[:, ls]
        ext = jnp.concatenate([halo_ref[:, ls], ug], axis=0)
        acc = ext
        span = 1
        while span < w:
            acc = acc + pltpu.roll(acc, span, 0)
            span *= 2
        win = acc[POOL_HALO:, :]
        count = jnp.minimum(t_idx + 1, w).astype(F32)
        d_ref[0, :, ls] = (win / count - ug).astype(d_ref.dtype)
    halo_ref[...] = u[rows - POOL_HALO:, :]


def _mixer_in(l, x, sh, sc, g, w_in, qg, kg):
    B, S, D = x.shape
    rows = min(IN_ROWS, S)
    o_spec = pl.BlockSpec((1, rows, ATTN_WIDTH), lambda b, j: (b, j, 0))
    o_shape = jax.ShapeDtypeStruct((B, S, ATTN_WIDTH), BF16)
    per_b = pl.BlockSpec((1, 1, D), lambda b, j: (b, 0, 0))
    return pl.pallas_call(
        _mixer_in_kernel,
        grid=(B, S // rows),
        in_specs=[
            pl.BlockSpec((1, rows, D), lambda b, j: (b, j, 0)),
            per_b, per_b, _layer_spec(g, l), _layer_spec(w_in, l), _layer_spec(qg, l), _layer_spec(kg, l),
        ],
        out_specs=[o_spec, o_spec, o_spec, o_spec],
        out_shape=[o_shape, o_shape, o_shape, o_shape],
        scratch_shapes=[pltpu.VMEM((POOL_HALO, POOL_WIDTH), F32)],
        compiler_params=_cparams(("arbitrary", "arbitrary")),
        name="mixer_in",
    )(x, sh, sc, g, w_in, qg, kg)


def _attn_kernel(q_ref, k0_ref, k1_ref, k2_ref, v0_ref, v1_ref, v2_ref, diag_ref, o_ref, bias_ref):
    i = pl.program_id(1)
    tq = q_ref.shape[1]
    k_refs = (k0_ref, k1_ref, k2_ref)
    v_refs = (v0_ref, v1_ref, v2_ref)
    nt = len(k_refs)

    @pl.when((pl.program_id(0) == 0) & (i == 0))
    def _():
        row = lax.broadcasted_iota(jnp.int32, (tq, nt * tq), 0)
        col = lax.broadcasted_iota(jnp.int32, (tq, nt * tq), 1)
        in_band = col - (row // CHUNK) * CHUNK
        ok = (in_band >= 0) & (in_band < BAND * CHUNK)
        width = diag_ref.shape[1]
        for h in range(N_HEADS):
            table = jnp.broadcast_to(diag_ref[h:h + 1, :], (tq, width))
            toeplitz = pltpu.roll(table, width - tq, 1, stride=1, stride_axis=0)[:, :nt * tq]
            bias_ref[h // 2, (h % 2) * tq:(h % 2 + 1) * tq, :] = jnp.where(ok, toeplitz, MASK_VALUE)

    lane = lax.broadcasted_iota(jnp.int32, (tq, LANES), 1)
    first = lane < HEAD_DIM
    nt_dims = (((1,), (1,)), ((), ()))

    kpos = lax.broadcasted_iota(jnp.int32, (2 * tq, nt * tq), 1)
    in_seq = kpos >= (nt - 1 - i) * tq
    for p in range(ATTN_WIDTH // LANES):
        ls = slice(p * LANES, (p + 1) * LANES)
        q = q_ref[0, :, ls]
        zero = jnp.zeros_like(q)
        q2 = jnp.concatenate([jnp.where(first, q, zero), jnp.where(first, zero, q)], axis=0)
        s = jnp.concatenate(
            [lax.dot_general(q2, kr[0, :, ls], nt_dims, preferred_element_type=F32) for kr in k_refs], axis=1)
        s = jnp.where(in_seq, s + bias_ref[p], MASK_VALUE)
        m = jnp.max(s, axis=-1, keepdims=True)
        e = jnp.exp(s - m)
        l = jnp.sum(e, axis=-1, keepdims=True)
        eb = e.astype(BF16)
        pv = jnp.dot(eb[:, :tq], v_refs[0][0, :, ls], preferred_element_type=F32)
        for n in range(1, nt):
            pv = pv + jnp.dot(eb[:, n * tq:(n + 1) * tq], v_refs[n][0, :, ls], preferred_element_type=F32)
        pv = pv / l
        o_ref[0, :, ls] = jnp.where(first, pv[:tq], pv[tq:]).astype(o_ref.dtype)


def _attention(q, k, v, diag):
    B, S, W = q.shape
    tq = ATTN_CHUNKS * CHUNK
    assert LEFT_CHUNKS == 2 * ATTN_CHUNKS and S % tq == 0
    cur = pl.BlockSpec((1, tq, W), lambda b, i: (b, i, 0))
    back = lambda n: pl.BlockSpec((1, tq, W), lambda b, i: (b, jnp.maximum(i - n, 0), 0))
    return pl.pallas_call(
        _attn_kernel,
        grid=(B, S // tq),
        in_specs=[cur, back(2), back(1), cur, back(2), back(1), cur,
                  pl.BlockSpec(diag.shape, lambda b, i: (0, 0))],
        out_specs=cur,
        out_shape=jax.ShapeDtypeStruct((B, S, W), BF16),
        scratch_shapes=[pltpu.VMEM((N_HEADS // 2, 2 * tq, 3 * tq), F32)],
        compiler_params=_cparams(("arbitrary", "arbitrary")),
        name="attention",
    )(q, k, k, k, v, v, v, diag)


def _bias_diagonals(rel_bias):
    tq = ATTN_CHUNKS * CHUNK
    H = rel_bias.shape[0]
    lo = tq + LEFT_CHUNKS * CHUNK - MAX_REL
    hi = 4 * tq - lo - (2 * MAX_REL + 1)
    return jnp.concatenate([jnp.broadcast_to(rel_bias[:, :1], (H, lo)), rel_bias,
                            jnp.broadcast_to(rel_bias[:, -1:], (H, hi))], axis=1).astype(F32)


def _mixer_out_kernel(x_ref, a_ref, d_ref, pw_ref, pb_ref, ps_ref, wo_ref, g1_ref, n2_ref, sh_ref, sc_ref,
                      rw_ref, rb_ref, x1_ref, h2_ref, route_ref, cnt_ref, run_ref):
    step = pl.program_id(0) * pl.num_programs(1) + pl.program_id(1)
    rows = x_ref.shape[1]

    @pl.when(step == 0)
    def _():
        run_ref[...] = jnp.zeros_like(run_ref)

    mixed = [a_ref[0]]
    for gi in range(len(POOL_WINDOWS)):
        ls = slice(gi * POOL_GROUP, (gi + 1) * POOL_GROUP)
        pg = jnp.dot(d_ref[0, :, ls], pw_ref[0, gi], preferred_element_type=F32)
        mixed.append(((pg + pb_ref[0, :, ls]) * ps_ref[0, :, ls]).astype(BF16))
    y = jnp.dot(jnp.concatenate(mixed, axis=1), wo_ref[0], preferred_element_type=F32)
    x1 = x_ref[0] + g1_ref[0] * y
    x1_ref[0] = x1

    ms = jnp.mean(x1 * x1, axis=-1, keepdims=True)
    h2 = (x1 * lax.rsqrt(ms + NORM_EPS) * n2_ref[0]) * (1.0 + sc_ref[0]) + sh_ref[0]
    for s in range(TOKEN_TILE):
        h2_ref[pl.ds(s, rows, stride=TOKEN_TILE), :] = h2[:, s * LANES:(s + 1) * LANES]

    logits = jnp.dot(h2.astype(BF16), rw_ref[0], preferred_element_type=F32) + rb_ref[0]
    ne = logits.shape[-1]
    elane = lax.broadcasted_iota(jnp.int32, (rows, ne), 1)
    work = logits
    picked = jnp.zeros((rows, ne), F32)
    vals, idxs = [], []
    for _ in range(TOP_K):
        m = jnp.max(work, axis=-1, keepdims=True)
        idx = jnp.min(jnp.where(work == m, elane, ne), axis=-1, keepdims=True)
        sel = elane == idx
        vals.append(m)
        idxs.append(idx)
        work = jnp.where(sel, -jnp.inf, work)
        picked = picked + sel.astype(F32)
    exps = [jnp.exp(v - vals[0]) for v in vals]
    denom = exps[0] + exps[1] + exps[2] + exps[3]

    r_i = lax.broadcasted_iota(jnp.int32, (rows, rows), 0)
    c_i = lax.broadcasted_iota(jnp.int32, (rows, rows), 1)
    tri = (c_i < r_i).astype(BF16)
    before = jnp.dot(tri, picked.astype(BF16), preferred_element_type=F32) + run_ref[...]
    run_ref[...] = run_ref[...] + jnp.sum(picked, axis=0, keepdims=True)

    lane = lax.broadcasted_iota(jnp.int32, (rows, LANES), 1)
    route = jnp.zeros((rows, LANES), F32)
    for kk in range(TOP_K):
        rank = jnp.sum(jnp.where(elane == idxs[kk], before, 0.0), axis=-1, keepdims=True)
        route = jnp.where(lane == kk, idxs[kk].astype(F32), route)
        route = jnp.where(lane == TOP_K + kk, exps[kk] / denom, route)
        route = jnp.where(lane == 2 * TOP_K + kk, rank, route)
    route_ref[0] = route
    cnt_ref[...] = jnp.broadcast_to(run_ref[...], cnt_ref.shape)


def _mixer_out(l, x, attn, d, pool_w, pool_b, pool_scale, w_out, g1, n2, sh2, sc2, router_w, router_b):
    B, S, D = x.shape
    rows = min(OUT_ROWS, S)
    E = router_w.shape[-1]
    tile = lambda w: pl.BlockSpec((1, rows, w), lambda b, j: (b, j, 0))
    per_b = pl.BlockSpec((1, 1, D), lambda b, j: (b, 0, 0))
    lay = lambda a: _layer_spec(a, l)
    return pl.pallas_call(
        _mixer_out_kernel,
        grid=(B, S // rows),
        in_specs=[
            tile(D), tile(ATTN_WIDTH), tile(POOL_WIDTH),
            lay(pool_w), lay(pool_b), lay(pool_scale), lay(w_out),
            per_b, lay(n2), per_b, per_b, lay(router_w), lay(router_b),
        ],
        out_specs=[tile(D), pl.BlockSpec((rows * TOKEN_TILE, LANES), lambda b, j: (b * (S // rows) + j, 0)),
                   tile(LANES), pl.BlockSpec((8, E), lambda b, j: (0, 0))],
        out_shape=[
            jax.ShapeDtypeStruct((B, S, D), F32),
            jax.ShapeDtypeStruct((B * S * TOKEN_TILE, LANES), F32),
            jax.ShapeDtypeStruct((B, S, LANES), F32),
            jax.ShapeDtypeStruct((8, E), F32),
        ],
        scratch_shapes=[pltpu.VMEM((1, E), F32)],
        compiler_params=_cparams(("arbitrary", "arbitrary")),
        name="mixer_out",
    )(x, attn, d, pool_w, pool_b, pool_scale, w_out, g1, n2, sh2, sc2, router_w, router_b)


def _experts_kernel(be_ref, first_ref, nused_ref, wslot_ref, next_e_ref,
                    tok_cur_ref, tok_n1_ref, tok_n2_ref, dst_prev_ref, dst_cur_ref,
                    h_ref, w1_ref, b1_ref, w2_ref, b2_ref, y_ref,
                    x0_ref, x1_ref, x2_ref, o0_ref, o1_ref, o2_ref, w1f_ref, w2f_ref, w1s_ref, w2s_ref,
                    gsem, ssem, zsem, wsem, *, layer, dump_start, n_dump):
    i = pl.program_id(0)
    nused = nused_ref[0]
    rows = MOE_BLOCK
    tr = TOKEN_TILE
    f = w2_ref.shape[2]

    def tile_in(tok, r, xbuf, sem):
        return pltpu.make_async_copy(h_ref.at[pl.ds(pl.multiple_of(tok * tr, tr), tr)], xbuf.at[pl.ds(r * tr, tr)], sem)

    def tile_out(obuf, r, dst, sem):
        return pltpu.make_async_copy(obuf.at[pl.ds(r * tr, tr)], y_ref.at[pl.ds(pl.multiple_of(dst * tr, tr), tr)], sem)

    def wait_in(xbuf, sem):
        pltpu.make_async_copy(h_ref.at[pl.ds(0, rows * tr)], xbuf, sem).wait()

    def wait_out(obuf, sem):
        pltpu.make_async_copy(obuf, y_ref.at[pl.ds(0, rows * tr)], sem).wait()

    xs = (x0_ref, x1_ref, x2_ref)
    os_ = (o0_ref, o1_ref, o2_ref)

    def gather_loop(tok_ref, xbuf, sem):
        def body(r, carry):
            tile_in(tok_ref[0, 0, r], r, xbuf, sem).start()
            return carry
        lax.fori_loop(0, rows, body, 0, unroll=8)

    def step(m):
        p1, p2 = (m + 2) % 3, (m + 1) % 3
        xcur, ocur, oprev = xs[m], os_[m], os_[p1]

        if m == 0:
            @pl.when(i == 0)
            def _():
                o0_ref[...] = jnp.zeros_like(o0_ref)
                o2_ref[...] = jnp.zeros_like(o2_ref)
                for c in range(n_dump):
                    pltpu.make_async_copy(o0_ref, y_ref.at[pl.ds((dump_start + c * rows) * tr, rows * tr)], zsem).start()
                for c in range(n_dump):
                    pltpu.make_async_copy(o0_ref, y_ref.at[pl.ds((dump_start + c * rows) * tr, rows * tr)], zsem).wait()
                gather_loop(tok_cur_ref, x0_ref, gsem.at[0])
                gather_loop(tok_n1_ref, x1_ref, gsem.at[1])

        wait_in(xcur, gsem.at[m])

        @pl.when(i >= 2)
        def _():
            wait_out(ocur, ssem.at[m])

        @pl.when(first_ref[i] == 1)
        def _():
            slot = wslot_ref[i]

            def weights_in(e, sl):
                return (pltpu.make_async_copy(w1_ref.at[layer, e], w1f_ref.at[sl], wsem.at[0, sl]),
                        pltpu.make_async_copy(w2_ref.at[layer, e], w2f_ref.at[sl], wsem.at[1, sl]))

            @pl.when(i == 0)
            def _():
                for cp in weights_in(be_ref[0], 0):
                    cp.start()

            for cp in weights_in(be_ref[i], slot):
                cp.wait()

            @pl.when(next_e_ref[i] < N_EXPERTS)
            def _():
                for cp in weights_in(next_e_ref[i], 1 - slot):
                    cp.start()

            w1s_ref[...] = w1f_ref[slot].astype(BF16)
            w2s_ref[...] = w2f_ref[slot].astype(BF16)

        for r in range(rows):
            tile_out(oprev, r, dst_prev_ref[0, 0, r], ssem.at[p1]).start()
        for r in range(rows):
            tile_in(tok_n2_ref[0, 0, r], r, xs[p1], gsem.at[p1]).start()
        x = jnp.concatenate([xcur[pl.ds(s, rows, stride=tr), :] for s in range(tr)], axis=1).astype(BF16)
        gu = jnp.dot(x, w1s_ref[...], preferred_element_type=F32) + b1_ref[0, 0]
        glu = jnp.minimum(gu[:, :f], SWIGLU_LIMIT)
        lin = jnp.clip(gu[:, f:], -SWIGLU_LIMIT, SWIGLU_LIMIT)
        act = glu * jax.nn.sigmoid(SWIGLU_ALPHA * glu) * (lin + 1.0)
        o = jnp.dot(act.astype(BF16), w2s_ref[...], preferred_element_type=F32) + b2_ref[0, 0]
        for s in range(tr):
            ocur[pl.ds(s, rows, stride=tr), :] = o[:, s * LANES:(s + 1) * LANES]

        @pl.when(i == nused - 1)
        def _():
            def last_scatter(r, carry):
                tile_out(ocur, r, dst_cur_ref[0, 0, r], ssem.at[m]).start()
                return carry
            lax.fori_loop(0, rows, last_scatter, 0, unroll=8)
            wait_out(os_[p2], ssem.at[p2])
            wait_out(oprev, ssem.at[p1])
            wait_out(ocur, ssem.at[m])
            wait_in(xs[p2], gsem.at[p2])
            wait_in(xs[p1], gsem.at[p1])

    for m in range(3):
        @pl.when((i < nused) & (i % 3 == m))
        def _(m=m):
            step(m)


def _experts(l, h2t, w1, b1, w2, b2, block_e, first, nused, wslot, next_e, tok3, dst3, n_rows, dump_start, n_dump):
    D, F2 = w1.shape[2:]
    F = w2.shape[2]
    nb = tok3.shape[0]
    idx_blk = (1, 1, MOE_BLOCK)
    smem = lambda fn: pl.BlockSpec(idx_blk, fn, memory_space=pltpu.SMEM)
    by_e = lambda i, be, *_: (l, be[i], 0, 0)
    block_rows = MOE_BLOCK * TOKEN_TILE
    grid_spec = pltpu.PrefetchScalarGridSpec(
        num_scalar_prefetch=5,
        grid=(nb,),
        in_specs=[
            smem(lambda i, *_: (i, 0, 0)),
            smem(lambda i, *_: (jnp.minimum(i + 1, nb - 1), 0, 0)),
            smem(lambda i, *_: (jnp.minimum(i + 2, nb - 1), 0, 0)),
            smem(lambda i, *_: (jnp.where(i == 0, nb, i - 1), 0, 0)),
            smem(lambda i, *_: (i, 0, 0)),
            pl.BlockSpec(memory_space=pl.ANY),
            pl.BlockSpec(memory_space=pl.ANY),
            pl.BlockSpec((1, 1, 1, F2), by_e),
            pl.BlockSpec(memory_space=pl.ANY),
            pl.BlockSpec((1, 1, 1, D), by_e),
        ],
        out_specs=pl.BlockSpec(memory_space=pl.ANY),
        scratch_shapes=[pltpu.VMEM((block_rows, LANES), F32)] * 6 + [
            pltpu.VMEM((2, D, F2), F32), pltpu.VMEM((2, F, D), F32),
            pltpu.VMEM((D, F2), BF16), pltpu.VMEM((F, D), BF16),
            pltpu.SemaphoreType.DMA((3,)), pltpu.SemaphoreType.DMA((3,)), pltpu.SemaphoreType.DMA(()),
            pltpu.SemaphoreType.DMA((2, 2))],
    )
    return pl.pallas_call(
        functools.partial(_experts_kernel, layer=l, dump_start=dump_start, n_dump=n_dump),
        grid_spec=grid_spec,
        out_shape=jax.ShapeDtypeStruct((n_rows * TOKEN_TILE, LANES), F32),
        compiler_params=_cparams(("arbitrary",)),
        name="experts",
    )(block_e, first, nused, wslot, next_e, tok3, tok3, tok3, dst3, dst3, h2t, w1, b1, w2, b2)


def _combine_kernel(y0_ref, y1_ref, y2_ref, y3_ref, route_ref, x_ref, g2_ref, o_ref):
    rows = x_ref.shape[0]
    route = route_ref[...]
    y = jnp.zeros(x_ref.shape, F32)
    for kk, yk_ref in enumerate((y0_ref, y1_ref, y2_ref, y3_ref)):
        yk = jnp.concatenate([yk_ref[pl.ds(s, rows, stride=TOKEN_TILE), :] for s in range(TOKEN_TILE)], axis=1)
        y = y + route[:, TOP_K + kk:TOP_K + kk + 1] * yk
    o_ref[...] = x_ref[...] + g2_ref[0] * y


def _combine(ybuf, route, x1, g2, seq):
    N, D = x1.shape
    rows = min(TOK_ROWS, N)
    steps = N // rows
    per_seq = seq // rows
    y_spec = lambda kk: pl.BlockSpec((rows * TOKEN_TILE, LANES), lambda i: (kk * steps + i, 0))
    return pl.pallas_call(
        _combine_kernel,
        grid=(steps,),
        in_specs=[y_spec(kk) for kk in range(TOP_K)] + [
            pl.BlockSpec((rows, LANES), lambda i: (i, 0)),
            pl.BlockSpec((rows, D), lambda i: (i, 0)),
            pl.BlockSpec((1, 1, D), lambda i: (i // per_seq, 0, 0)),
        ],
        out_specs=pl.BlockSpec((rows, D), lambda i: (i, 0)),
        out_shape=jax.ShapeDtypeStruct((N, D), F32),
        compiler_params=_cparams(("arbitrary",)),
        name="combine",
    )(ybuf, ybuf, ybuf, ybuf, route, x1, g2)


def _slots(route, counts, n_tokens):
    idx = route[:, :TOP_K].astype(jnp.int32)
    rank = route[:, 2 * TOP_K:3 * TOP_K].astype(jnp.int32)
    counts = counts.astype(jnp.int32)
    padded = (counts + MOE_BLOCK - 1) // MOE_BLOCK * MOE_BLOCK
    pend = jnp.cumsum(padded)
    pstart = pend - padded
    experts = jnp.arange(N_EXPERTS, dtype=jnp.int32)
    dest = rank + jnp.sum(jnp.where(idx[..., None] == experts, pstart, 0), axis=-1)
    n_pairs = n_tokens * TOP_K
    n_pad = N_EXPERTS * MOE_BLOCK
    n_blocks = (n_pairs + n_pad) // MOE_BLOCK
    nused = (pend[-1] // MOE_BLOCK).astype(jnp.int32)
    blk = jnp.minimum(jnp.arange(n_blocks, dtype=jnp.int32), nused - 1)
    block_e = jnp.sum((blk[:, None] * MOE_BLOCK >= pend[None, :]).astype(jnp.int32), axis=-1)
    first = jnp.concatenate([jnp.ones((1,), jnp.int32), (block_e[1:] != block_e[:-1]).astype(jnp.int32)])
    wslot = (jnp.cumsum(first) - 1) % 2
    later = (experts[None, :] > block_e[:, None]) & (padded > 0)[None, :]
    next_e = jnp.min(jnp.where(later, experts[None, :], N_EXPERTS), axis=-1).astype(jnp.int32)

    j = jnp.arange(MOE_BLOCK, dtype=jnp.int32)
    pad_key = jnp.where(j[None, :] < (padded - counts)[:, None], (pstart + counts)[:, None] + j[None, :], jnp.int32(2 ** 30))
    keys = jnp.concatenate([dest.reshape(-1), pad_key.reshape(-1)])
    ids = jnp.arange(n_pairs + n_pad, dtype=jnp.int32)
    _, src = lax.sort((keys, ids), num_keys=1)
    real = src < n_pairs
    tok = jnp.where(real, src // TOP_K, 0)
    dst = jnp.where(real, (src % TOP_K) * n_tokens + src // TOP_K, src)
    spare = n_pairs + n_pad + j
    tok3 = tok.reshape(n_blocks, 1, MOE_BLOCK)
    dst3 = jnp.concatenate([dst, spare]).reshape(n_blocks + 1, 1, MOE_BLOCK)
    n_rows = n_pairs + n_pad + MOE_BLOCK
    return block_e, first, nused.reshape(1), wslot.astype(jnp.int32), next_e, tok3, dst3, n_rows


def kernel(x, c, norm1_g, w_mod, b_mod, w_in, q_norm_g, k_norm_g, rel_bias, pool_w, pool_b, pool_scale, w_out,
           norm2_g, router_w, router_b, exp_w1, exp_b1, exp_w2, exp_b2):
    B, S, D = x.shape
    L = w_mod.shape[0]
    N = B * S
    E = router_w.shape[-1]
    row3 = lambda a: a.reshape(L, 1, -1)
    mod = _modulation(c, w_mod, b_mod)
    w_in_b, w_out_b, pool_w_b, router_w_b = (a.astype(BF16) for a in (w_in, w_out, pool_w, router_w))
    qg = row3(jnp.tile(q_norm_g, (1, N_HEADS)))
    kg = row3(jnp.tile(k_norm_g, (1, N_HEADS)))
    n1, n2, pb, ps, rb = row3(norm1_g), row3(norm2_g), row3(pool_b), row3(pool_scale), row3(router_b)
    b1 = exp_b1.reshape(L, E, 1, -1)
    b2 = exp_b2.reshape(L, E, 1, -1)
    for l in range(L):
        sh1, sc1, g1, sh2, sc2, g2 = [mod[l, :, i * D:(i + 1) * D].reshape(B, 1, D) for i in range(6)]
        q, k, v, d = _mixer_in(l, x, sh1, sc1, n1, w_in_b, qg, kg)
        attn = _attention(q, k, v, _bias_diagonals(rel_bias[l]))
        x1, h2t, route, counts = _mixer_out(l, x, attn, d, pool_w_b, pb, ps, w_out_b, g1, n2, sh2, sc2, router_w_b, rb)
        route = route.reshape(N, LANES)
        block_e, first, nused, wslot, next_e, tok3, dst3, n_rows = _slots(route, counts[0], N)
        ybuf = _experts(l, h2t, exp_w1, b1, exp_w2, b2, block_e, first, nused, wslot, next_e, tok3, dst3, n_rows,
                        dump_start=N * TOP_K, n_dump=N_EXPERTS)
        x = _combine(ybuf, route, x1.reshape(N, D), g2, S).reshape(B, S, D)
    return x
```

```python
import functools

import jax
import jax.numpy as jnp
from jax import lax
from jax.experimental import pallas as pl
from jax.experimental.pallas import tpu as pltpu

F32 = jnp.float32
BF16 = jnp.bfloat16

CHUNK = 64
LEFT_CHUNKS = 8
BAND = LEFT_CHUNKS + 1
HEAD_DIM = 64
N_HEADS = 8
ATTN_WIDTH = N_HEADS * HEAD_DIM
MAX_REL = 2 * CHUNK
POOL_WINDOWS = (2, 4, 8, 16)
POOL_GROUP = 128
POOL_WIDTH = POOL_GROUP * len(POOL_WINDOWS)
N_EXPERTS = 32
TOP_K = 4
SWIGLU_ALPHA = 1.702
SWIGLU_LIMIT = 7.0
MOE_BLOCK = 256
NORM_EPS = 1e-6
MASK_VALUE = -1e30

LANES = 128
TOKEN_TILE = 8
POOL_HALO = 16

IN_ROWS = 1024
ATTN_CHUNKS = 4
OUT_ROWS = 1024
TOK_ROWS = 512
VMEM_LIMIT = 56 * 1024 * 1024


def _cparams(sem, vmem=VMEM_LIMIT):
    return pltpu.CompilerParams(dimension_semantics=sem, vmem_limit_bytes=vmem)


def _layer_spec(a, l):
    zeros = (0,) * (a.ndim - 1)
    return pl.BlockSpec((1,) + a.shape[1:], lambda *_: (l,) + zeros)


def _mod_kernel(c_ref, w_ref, b_ref, o_ref):
    c = c_ref[...]
    cond = c * jax.nn.sigmoid(c)
    o_ref[0] = jnp.dot(cond.astype(BF16), w_ref[0].astype(BF16), preferred_element_type=F32) + b_ref[0]


def _modulation(c, w_mod, b_mod):
    L, D, W = w_mod.shape
    B = c.shape[0]
    tn = 1024
    return pl.pallas_call(
        _mod_kernel,
        grid=(L, W // tn),
        in_specs=[
            pl.BlockSpec((B, D), lambda l, j: (0, 0)),
            pl.BlockSpec((1, D, tn), lambda l, j: (l, 0, j)),
            pl.BlockSpec((1, 1, tn), lambda l, j: (l, 0, j)),
        ],
        out_specs=pl.BlockSpec((1, B, tn), lambda l, j: (l, 0, j)),
        out_shape=jax.ShapeDtypeStruct((L, B, W), F32),
        compiler_params=_cparams(("arbitrary", "arbitrary")),
        name="modulation",
    )(c, w_mod, b_mod.reshape(L, 1, W))


def _head_norm(t, gain):
    rows = t.shape[0]
    lane = lax.broadcasted_iota(jnp.int32, (rows, LANES), 1)
    first = lane < HEAD_DIM
    outs = []
    for p in range(ATTN_WIDTH // LANES):
        blk = t[:, p * LANES:(p + 1) * LANES]
        sq = blk * blk
        sa = jnp.sum(jnp.where(first, sq, 0.0), axis=-1, keepdims=True)
        sb = jnp.sum(jnp.where(first, 0.0, sq), axis=-1, keepdims=True)
        ra = lax.rsqrt(sa * (1.0 / HEAD_DIM) + NORM_EPS)
        rb = lax.rsqrt(sb * (1.0 / HEAD_DIM) + NORM_EPS)
        outs.append(blk * jnp.where(first, ra, rb))
    return jnp.concatenate(outs, axis=-1) * gain


def _mixer_in_kernel(x_ref, sh_ref, sc_ref, g_ref, w_ref, qg_ref, kg_ref,
                     q_ref, k_ref, v_ref, d_ref, halo_ref):
    j = pl.program_id(1)
    rows = x_ref.shape[1]
    x = x_ref[0]
    ms = jnp.mean(x * x, axis=-1, keepdims=True)
    h = (x * lax.rsqrt(ms + NORM_EPS) * g_ref[0]) * (1.0 + sc_ref[0]) + sh_ref[0]
    z = jnp.dot(h.astype(BF16), w_ref[0], preferred_element_type=F32)
    aw = ATTN_WIDTH
    q_ref[0] = (_head_norm(z[:, :aw], qg_ref[0]) * (HEAD_DIM ** -0.5)).astype(q_ref.dtype)
    k_ref[0] = _head_norm(z[:, aw:2 * aw], kg_ref[0]).astype(k_ref.dtype)
    v_ref[0] = z[:, 2 * aw:3 * aw].astype(v_ref.dtype)

    u = z[:, 3 * aw:]

    @pl.when(j == 0)
    def _():
        halo_ref[...] = jnp.zeros_like(halo_ref)

    t_idx = j * rows + lax.broadcasted_iota(jnp.int32, (rows, 1), 0)
    for gi, w in enumerate(POOL_WINDOWS):
        ls = slice(gi * POOL_GROUP, (gi + 1) * POOL_GROUP)
        ug = u[:, ls]
        ext = jnp.concatenate([halo_ref[:, ls], ug], axis=0)
        acc = ext
        span = 1
        while span < w:
            acc = acc + pltpu.roll(acc, span, 0)
            span *= 2
        win = acc[POOL_HALO:, :]
        count = jnp.minimum(t_idx + 1, w).astype(F32)
        d_ref[0, :, ls] = (win / count - ug).astype(d_ref.dtype)
    halo_ref[...] = u[rows - POOL_HALO:, :]


def _mixer_in(l, x, sh, sc, g, w_in, qg, kg):
    B, S, D = x.shape
    rows = min(IN_ROWS, S)
    o_spec = pl.BlockSpec((1, rows, ATTN_WIDTH), lambda b, j: (b, j, 0))
    o_shape = jax.ShapeDtypeStruct((B, S, ATTN_WIDTH), BF16)
    per_b = pl.BlockSpec((1, 1, D), lambda b, j: (b, 0, 0))
    return pl.pallas_call(
        _mixer_in_kernel,
        grid=(B, S // rows),
        in_specs=[
            pl.BlockSpec((1, rows, D), lambda b, j: (b, j, 0)),
            per_b, per_b, _layer_spec(g, l), _layer_spec(w_in, l), _layer_spec(qg, l), _layer_spec(kg, l),
        ],
        out_specs=[o_spec, o_spec, o_spec, o_spec],
        out_shape=[o_shape, o_shape, o_shape, o_shape],
        scratch_shapes=[pltpu.VMEM((POOL_HALO, POOL_WIDTH), F32)],
        compiler_params=_cparams(("arbitrary", "arbitrary")),
        name="mixer_in",
    )(x, sh, sc, g, w_in, qg, kg)


def _attn_kernel(q_ref, k0_ref, k1_ref, k2_ref, v0_ref, v1_ref, v2_ref, diag_ref, o_ref, bias_ref):
    i = pl.program_id(1)
    tq = q_ref.shape[1]
    k_refs = (k0_ref, k1_ref, k2_ref)
    v_refs = (v0_ref, v1_ref, v2_ref)
    nt = len(k_refs)

    @pl.when((pl.program_id(0) == 0) & (i == 0))
    def _():
        row = lax.broadcasted_iota(jnp.int32, (tq, nt * tq), 0)
        col = lax.broadcasted_iota(jnp.int32, (tq, nt * tq), 1)
        in_band = col - (row // CHUNK) * CHUNK
        ok = (in_band >= 0) & (in_band < BAND * CHUNK)
        width = diag_ref.shape[1]
        for h in range(N_HEADS):
            table = jnp.broadcast_to(diag_ref[h:h + 1, :], (tq, width))
            toeplitz = pltpu.roll(table, width - tq, 1, stride=1, stride_axis=0)[:, :nt * tq]
            bias_ref[h // 2, (h % 2) * tq:(h % 2 + 1) * tq, :] = jnp.where(ok, toeplitz, MASK_VALUE)

    lane = lax.broadcasted_iota(jnp.int32, (tq, LANES), 1)
    first = lane < HEAD_DIM
    nt_dims = (((1,), (1,)), ((), ()))

    kpos = lax.broadcasted_iota(jnp.int32, (2 * tq, nt * tq), 1)
    in_seq = kpos >= (nt - 1 - i) * tq
    for p in range(ATTN_WIDTH // LANES):
        ls = slice(p * LANES, (p + 1) * LANES)
        q = q_ref[0, :, ls]
        zero = jnp.zeros_like(q)
        q2 = jnp.concatenate([jnp.where(first, q, zero), jnp.where(first, zero, q)], axis=0)
        s = jnp.concatenate(
            [lax.dot_general(q2, kr[0, :, ls], nt_dims, preferred_element_type=F32) for kr in k_refs], axis=1)
        s = jnp.where(in_seq, s + bias_ref[p], MASK_VALUE)
        m = jnp.max(s, axis=-1, keepdims=True)
        e = jnp.exp(s - m)
        l = jnp.sum(e, axis=-1, keepdims=True)
        eb = e.astype(BF16)
        pv = jnp.dot(eb[:, :tq], v_refs[0][0, :, ls], preferred_element_type=F32)
        for n in range(1, nt):
            pv = pv + jnp.dot(eb[:, n * tq:(n + 1) * tq], v_refs[n][0, :, ls], preferred_element_type=F32)
        pv = pv / l
        o_ref[0, :, ls] = jnp.where(first, pv[:tq], pv[tq:]).astype(o_ref.dtype)


def _attention(q, k, v, diag):
    B, S, W = q.shape
    tq = ATTN_CHUNKS * CHUNK
    assert LEFT_CHUNKS == 2 * ATTN_CHUNKS and S % tq == 0
    cur = pl.BlockSpec((1, tq, W), lambda b, i: (b, i, 0))
    back = lambda n: pl.BlockSpec((1, tq, W), lambda b, i: (b, jnp.maximum(i - n, 0), 0))
    return pl.pallas_call(
        _attn_kernel,
        grid=(B, S // tq),
        in_specs=[cur, back(2), back(1), cur, back(2), back(1), cur,
                  pl.BlockSpec(diag.shape, lambda b, i: (0, 0))],
        out_specs=cur,
        out_shape=jax.ShapeDtypeStruct((B, S, W), BF16),
        scratch_shapes=[pltpu.VMEM((N_HEADS // 2, 2 * tq, 3 * tq), F32)],
        compiler_params=_cparams(("arbitrary", "arbitrary")),
        name="attention",
    )(q, k, k, k, v, v, v, diag)


def _bias_diagonals(rel_bias):
    tq = ATTN_CHUNKS * CHUNK
    H = rel_bias.shape[0]
    lo = tq + LEFT_CHUNKS * CHUNK - MAX_REL
    hi = 4 * tq - lo - (2 * MAX_REL + 1)
    return jnp.concatenate([jnp.broadcast_to(rel_bias[:, :1], (H, lo)), rel_bias,
                            jnp.broadcast_to(rel_bias[:, -1:], (H, hi))], axis=1).astype(F32)


def _mixer_out_kernel(x_ref, a_ref, d_ref, pw_ref, pb_ref, ps_ref, wo_ref, g1_ref, n2_ref, sh_ref, sc_ref,
                      rw_ref, rb_ref, x1_ref, h2_ref, route_ref, cnt_ref, run_ref):
    step = pl.program_id(0) * pl.num_programs(1) + pl.program_id(1)
    rows = x_ref.shape[1]

    @pl.when(step == 0)
    def _():
        run_ref[...] = jnp.zeros_like(run_ref)

    mixed = [a_ref[0]]
    for gi in range(len(POOL_WINDOWS)):
        ls = slice(gi * POOL_GROUP, (gi + 1) * POOL_GROUP)
        pg = jnp.dot(d_ref[0, :, ls], pw_ref[0, gi], preferred_element_type=F32)
        mixed.append(((pg + pb_ref[0, :, ls]) * ps_ref[0, :, ls]).astype(BF16))
    y = jnp.dot(jnp.concatenate(mixed, axis=1), wo_ref[0], preferred_element_type=F32)
    x1 = x_ref[0] + g1_ref[0] * y
    x1_ref[0] = x1

    ms = jnp.mean(x1 * x1, axis=-1, keepdims=True)
    h2 = (x1 * lax.rsqrt(ms + NORM_EPS) * n2_ref[0]) * (1.0 + sc_ref[0]) + sh_ref[0]
    for s in range(TOKEN_TILE):
        h2_ref[pl.ds(s, rows, stride=TOKEN_TILE), :] = h2[:, s * LANES:(s + 1) * LANES]

    logits = jnp.dot(h2.astype(BF16), rw_ref[0], preferred_element_type=F32) + rb_ref[0]
    ne = logits.shape[-1]
    elane = lax.broadcasted_iota(jnp.int32, (rows, ne), 1)
    work = logits
    picked = jnp.zeros((rows, ne), F32)
    vals, idxs = [], []
    for _ in range(TOP_K):
        m = jnp.max(work, axis=-1, keepdims=True)
        idx = jnp.min(jnp.where(work == m, elane, ne), axis=-1, keepdims=True)
        sel = elane == idx
        vals.append(m)
        idxs.append(idx)
        work = jnp.where(sel, -jnp.inf, work)
        picked = picked + sel.astype(F32)
    exps = [jnp.exp(v - vals[0]) for v in vals]
    denom = exps[0] + exps[1] + exps[2] + exps[3]

    r_i = lax.broadcasted_iota(jnp.int32, (rows, rows), 0)
    c_i = lax.broadcasted_iota(jnp.int32, (rows, rows), 1)
    tri = (c_i < r_i).astype(BF16)
    before = jnp.dot(tri, picked.astype(BF16), preferred_element_type=F32) + run_ref[...]
    run_ref[...] = run_ref[...] + jnp.sum(picked, axis=0, keepdims=True)

    lane = lax.broadcasted_iota(jnp.int32, (rows, LANES), 1)
    route = jnp.zeros((rows, LANES), F32)
    for kk in range(TOP_K):
        rank = jnp.sum(jnp.where(elane == idxs[kk], before, 0.0), axis=-1, keepdims=True)
        route = jnp.where(lane == kk, idxs[kk].astype(F32), route)
        route = jnp.where(lane == TOP_K + kk, exps[kk] / denom, route)
        route = jnp.where(lane == 2 * TOP_K + kk, rank, route)
    route_ref[0] = route
    cnt_ref[...] = jnp.broadcast_to(run_ref[...], cnt_ref.shape)


def _mixer_out(l, x, attn, d, pool_w, pool_b, pool_scale, w_out, g1, n2, sh2, sc2, router_w, router_b):
    B, S, D = x.shape
    rows = min(OUT_ROWS, S)
    E = router_w.shape[-1]
    tile = lambda w: pl.BlockSpec((1, rows, w), lambda b, j: (b, j, 0))
    per_b = pl.BlockSpec((1, 1, D), lambda b, j: (b, 0, 0))
    lay = lambda a: _layer_spec(a, l)
    return pl.pallas_call(
        _mixer_out_kernel,
        grid=(B, S // rows),
        in_specs=[
            tile(D), tile(ATTN_WIDTH), tile(POOL_WIDTH),
            lay(pool_w), lay(pool_b), lay(pool_scale), lay(w_out),
            per_b, lay(n2), per_b, per_b, lay(router_w), lay(router_b),
        ],
        out_specs=[tile(D), pl.BlockSpec((rows * TOKEN_TILE, LANES), lambda b, j: (b * (S // rows) + j, 0)),
                   tile(LANES), pl.BlockSpec((8, E), lambda b, j: (0, 0))],
        out_shape=[
            jax.ShapeDtypeStruct((B, S, D), F32),
            jax.ShapeDtypeStruct((B * S * TOKEN_TILE, LANES), F32),
            jax.ShapeDtypeStruct((B, S, LANES), F32),
            jax.ShapeDtypeStruct((8, E), F32),
        ],
        scratch_shapes=[pltpu.VMEM((1, E), F32)],
        compiler_params=_cparams(("arbitrary", "arbitrary")),
        name="mixer_out",
    )(x, attn, d, pool_w, pool_b, pool_scale, w_out, g1, n2, sh2, sc2, router_w, router_b)


def _experts_kernel(be_ref, first_ref, nused_ref, wslot_ref, next_e_ref,
                    tok_cur_ref, tok_n1_ref, tok_n2_ref, dst_prev_ref, dst_cur_ref,
                    h_ref, w1_ref, b1_ref, w2_ref, b2_ref, y_ref,
                    x0_ref, x1_ref, x2_ref, o0_ref, o1_ref, o2_ref, w1f_ref, w2f_ref, w1s_ref, w2s_ref,
                    gsem, ssem, zsem, wsem, *, layer, dump_start, n_dump):
    i = pl.program_id(0)
    nused = nused_ref[0]
    rows = MOE_BLOCK
    tr = TOKEN_TILE
    f = w2_ref.shape[2]

    def tile_in(tok, r, xbuf, sem):
        return pltpu.make_async_copy(h_ref.at[pl.ds(pl.multiple_of(tok * tr, tr), tr)], xbuf.at[pl.ds(r * tr, tr)], sem)

    def tile_out(obuf, r, dst, sem):
        return pltpu.make_async_copy(obuf.at[pl.ds(r * tr, tr)], y_ref.at[pl.ds(pl.multiple_of(dst * tr, tr), tr)], sem)

    def wait_in(xbuf, sem):
        pltpu.make_async_copy(h_ref.at[pl.ds(0, rows * tr)], xbuf, sem).wait()

    def wait_out(obuf, sem):
        pltpu.make_async_copy(obuf, y_ref.at[pl.ds(0, rows * tr)], sem).wait()

    xs = (x0_ref, x1_ref, x2_ref)
    os_ = (o0_ref, o1_ref, o2_ref)

    def gather_loop(tok_ref, xbuf, sem):
        def body(r, carry):
            tile_in(tok_ref[0, 0, r], r, xbuf, sem).start()
            return carry
        lax.fori_loop(0, rows, body, 0, unroll=8)

    def step(m):
        p1, p2 = (m + 2) % 3, (m + 1) % 3
        xcur, ocur, oprev = xs[m], os_[m], os_[p1]

        if m == 0:
            @pl.when(i == 0)
            def _():
                o0_ref[...] = jnp.zeros_like(o0_ref)
                o2_ref[...] = jnp.zeros_like(o2_ref)
                for c in range(n_dump):
                    pltpu.make_async_copy(o0_ref, y_ref.at[pl.ds((dump_start + c * rows) * tr, rows * tr)], zsem).start()
                for c in range(n_dump):
                    pltpu.make_async_copy(o0_ref, y_ref.at[pl.ds((dump_start + c * rows) * tr, rows * tr)], zsem).wait()
                gather_loop(tok_cur_ref, x0_ref, gsem.at[0])
                gather_loop(tok_n1_ref, x1_ref, gsem.at[1])

        wait_in(xcur, gsem.at[m])

        @pl.when(i >= 2)
        def _():
            wait_out(ocur, ssem.at[m])

        @pl.when(first_ref[i] == 1)
        def _():
            slot = wslot_ref[i]

            def weights_in(e, sl):
                return (pltpu.make_async_copy(w1_ref.at[layer, e], w1f_ref.at[sl], wsem.at[0, sl]),
                        pltpu.make_async_copy(w2_ref.at[layer, e], w2f_ref.at[sl], wsem.at[1, sl]))

            @pl.when(i == 0)
            def _():
                for cp in weights_in(be_ref[0], 0):
                    cp.start()

            for cp in weights_in(be_ref[i], slot):
                cp.wait()

            @pl.when(next_e_ref[i] < N_EXPERTS)
            def _():
                for cp in weights_in(next_e_ref[i], 1 - slot):
                    cp.start()

            w1s_ref[...] = w1f_ref[slot].astype(BF16)
            w2s_ref[...] = w2f_ref[slot].astype(BF16)

        for r in range(rows):
            tile_out(oprev, r, dst_prev_ref[0, 0, r], ssem.at[p1]).start()
        for r in range(rows):
            tile_in(tok_n2_ref[0, 0, r], r, xs[p1], gsem.at[p1]).start()
        x = jnp.concatenate([xcur[pl.ds(s, rows, stride=tr), :] for s in range(tr)], axis=1).astype(BF16)
        gu = jnp.dot(x, w1s_ref[...], preferred_element_type=F32) + b1_ref[0, 0]
        glu = jnp.minimum(gu[:, :f], SWIGLU_LIMIT)
        lin = jnp.clip(gu[:, f:], -SWIGLU_LIMIT, SWIGLU_LIMIT)
        act = glu * jax.nn.sigmoid(SWIGLU_ALPHA * glu) * (lin + 1.0)
        o = jnp.dot(act.astype(BF16), w2s_ref[...], preferred_element_type=F32) + b2_ref[0, 0]
        for s in range(tr):
            ocur[pl.ds(s, rows, stride=tr), :] = o[:, s * LANES:(s + 1) * LANES]

        @pl.when(i == nused - 1)
        def _():
            def last_scatter(r, carry):
                tile_out(ocur, r, dst_cur_ref[0, 0, r], ssem.at[m]).start()
                return carry
            lax.fori_loop(0, rows, last_scatter, 0, unroll=8)
            wait_out(os_[p2], ssem.at[p2])
            wait_out(oprev, ssem.at[p1])
            wait_out(ocur, ssem.at[m])
            wait_in(xs[p2], gsem.at[p2])
            wait_in(xs[p1], gsem.at[p1])

    for m in range(3):
        @pl.when((i < nused) & (i % 3 == m))
        def _(m=m):
            step(m)


def _experts(l, h2t, w1, b1, w2, b2, block_e, first, nused, wslot, next_e, tok3, dst3, n_rows, dump_start, n_dump):
    D, F2 = w1.shape[2:]
    F = w2.shape[2]
    nb = tok3.shape[0]
    idx_blk = (1, 1, MOE_BLOCK)
    smem = lambda fn: pl.BlockSpec(idx_blk, fn, memory_space=pltpu.SMEM)
    by_e = lambda i, be, *_: (l, be[i], 0, 0)
    block_rows = MOE_BLOCK * TOKEN_TILE
    grid_spec = pltpu.PrefetchScalarGridSpec(
        num_scalar_prefetch=5,
        grid=(nb,),
        in_specs=[
            smem(lambda i, *_: (i, 0, 0)),
            smem(lambda i, *_: (jnp.minimum(i + 1, nb - 1), 0, 0)),
            smem(lambda i, *_: (jnp.minimum(i + 2, nb - 1), 0, 0)),
            smem(lambda i, *_: (jnp.where(i == 0, nb, i - 1), 0, 0)),
            smem(lambda i, *_: (i, 0, 0)),
            pl.BlockSpec(memory_space=pl.ANY),
            pl.BlockSpec(memory_space=pl.ANY),
            pl.BlockSpec((1, 1, 1, F2), by_e),
            pl.BlockSpec(memory_space=pl.ANY),
            pl.BlockSpec((1, 1, 1, D), by_e),
        ],
        out_specs=pl.BlockSpec(memory_space=pl.ANY),
        scratch_shapes=[pltpu.VMEM((block_rows, LANES), F32)] * 6 + [
            pltpu.VMEM((2, D, F2), F32), pltpu.VMEM((2, F, D), F32),
            pltpu.VMEM((D, F2), BF16), pltpu.VMEM((F, D), BF16),
            pltpu.SemaphoreType.DMA((3,)), pltpu.SemaphoreType.DMA((3,)), pltpu.SemaphoreType.DMA(()),
            pltpu.SemaphoreType.DMA((2, 2))],
    )
    return pl.pallas_call(
        functools.partial(_experts_kernel, layer=l, dump_start=dump_start, n_dump=n_dump),
        grid_spec=grid_spec,
        out_shape=jax.ShapeDtypeStruct((n_rows * TOKEN_TILE, LANES), F32),
        compiler_params=_cparams(("arbitrary",)),
        name="experts",
    )(block_e, first, nused, wslot, next_e, tok3, tok3, tok3, dst3, dst3, h2t, w1, b1, w2, b2)


def _combine_kernel(y0_ref, y1_ref, y2_ref, y3_ref, route_ref, x_ref, g2_ref, o_ref):
    rows = x_ref.shape[0]
    route = route_ref[...]
    y = jnp.zeros(x_ref.shape, F32)
    for kk, yk_ref in enumerate((y0_ref, y1_ref, y2_ref, y3_ref)):
        yk = jnp.concatenate([yk_ref[pl.ds(s, rows, stride=TOKEN_TILE), :] for s in range(TOKEN_TILE)], axis=1)
        y = y + route[:, TOP_K + kk:TOP_K + kk + 1] * yk
    o_ref[...] = x_ref[...] + g2_ref[0] * y


def _combine(ybuf, route, x1, g2, seq):
    N, D = x1.shape
    rows = min(TOK_ROWS, N)
    steps = N // rows
    per_seq = seq // rows
    y_spec = lambda kk: pl.BlockSpec((rows * TOKEN_TILE, LANES), lambda i: (kk * steps + i, 0))
    return pl.pallas_call(
        _combine_kernel,
        grid=(steps,),
        in_specs=[y_spec(kk) for kk in range(TOP_K)] + [
            pl.BlockSpec((rows, LANES), lambda i: (i, 0)),
            pl.BlockSpec((rows, D), lambda i: (i, 0)),
            pl.BlockSpec((1, 1, D), lambda i: (i // per_seq, 0, 0)),
        ],
        out_specs=pl.BlockSpec((rows, D), lambda i: (i, 0)),
        out_shape=jax.ShapeDtypeStruct((N, D), F32),
        compiler_params=_cparams(("arbitrary",)),
        name="combine",
    )(ybuf, ybuf, ybuf, ybuf, route, x1, g2)


def _slots(route, counts, n_tokens):
    idx = route[:, :TOP_K].astype(jnp.int32)
    rank = route[:, 2 * TOP_K:3 * TOP_K].astype(jnp.int32)
    counts = counts.astype(jnp.int32)
    padded = (counts + MOE_BLOCK - 1) // MOE_BLOCK * MOE_BLOCK
    pend = jnp.cumsum(padded)
    pstart = pend - padded
    experts = jnp.arange(N_EXPERTS, dtype=jnp.int32)
    dest = rank + jnp.sum(jnp.where(idx[..., None] == experts, pstart, 0), axis=-1)
    n_pairs = n_tokens * TOP_K
    n_pad = N_EXPERTS * MOE_BLOCK
    n_blocks = (n_pairs + n_pad) // MOE_BLOCK
    nused = (pend[-1] // MOE_BLOCK).astype(jnp.int32)
    blk = jnp.minimum(jnp.arange(n_blocks, dtype=jnp.int32), nused - 1)
    block_e = jnp.sum((blk[:, None] * MOE_BLOCK >= pend[None, :]).astype(jnp.int32), axis=-1)
    first = jnp.concatenate([jnp.ones((1,), jnp.int32), (block_e[1:] != block_e[:-1]).astype(jnp.int32)])
    wslot = (jnp.cumsum(first) - 1) % 2
    later = (experts[None, :] > block_e[:, None]) & (padded > 0)[None, :]
    next_e = jnp.min(jnp.where(later, experts[None, :], N_EXPERTS), axis=-1).astype(jnp.int32)

    j = jnp.arange(MOE_BLOCK, dtype=jnp.int32)
    pad_key = jnp.where(j[None, :] < (padded - counts)[:, None], (pstart + counts)[:, None] + j[None, :], jnp.int32(2 ** 30))
    keys = jnp.concatenate([dest.reshape(-1), pad_key.reshape(-1)])
    ids = jnp.arange(n_pairs + n_pad, dtype=jnp.int32)
    _, src = lax.sort((keys, ids), num_keys=1, is_stable=False)
    real = src < n_pairs
    tok = jnp.where(real, src // TOP_K, 0)
    dst = jnp.where(real, (src % TOP_K) * n_tokens + src // TOP_K, src)
    spare = n_pairs + n_pad + j
    tok3 = tok.reshape(n_blocks, 1, MOE_BLOCK)
    dst3 = jnp.concatenate([dst, spare]).reshape(n_blocks + 1, 1, MOE_BLOCK)
    n_rows = n_pairs + n_pad + MOE_BLOCK
    return block_e, first, nused.reshape(1), wslot.astype(jnp.int32), next_e, tok3, dst3, n_rows


def kernel(x, c, norm1_g, w_mod, b_mod, w_in, q_norm_g, k_norm_g, rel_bias, pool_w, pool_b, pool_scale, w_out,
           norm2_g, router_w, router_b, exp_w1, exp_b1, exp_w2, exp_b2):
    B, S, D = x.shape
    L = w_mod.shape[0]
    N = B * S
    E = router_w.shape[-1]
    row3 = lambda a: a.reshape(L, 1, -1)
    mod = _modulation(c, w_mod, b_mod)
    w_in_b, w_out_b, pool_w_b, router_w_b = (a.astype(BF16) for a in (w_in, w_out, pool_w, router_w))
    qg = row3(jnp.tile(q_norm_g, (1, N_HEADS)))
    kg = row3(jnp.tile(k_norm_g, (1, N_HEADS)))
    n1, n2, pb, ps, rb = row3(norm1_g), row3(norm2_g), row3(pool_b), row3(pool_scale), row3(router_b)
    b1 = exp_b1.reshape(L, E, 1, -1)
    b2 = exp_b2.reshape(L, E, 1, -1)
    for l in range(L):
        sh1, sc1, g1, sh2, sc2, g2 = [mod[l, :, i * D:(i + 1) * D].reshape(B, 1, D) for i in range(6)]
        q, k, v, d = _mixer_in(l, x, sh1, sc1, n1, w_in_b, qg, kg)
        attn = _attention(q, k, v, _bias_diagonals(rel_bias[l]))
        x1, h2t, route, counts = _mixer_out(l, x, attn, d, pool_w_b, pb, ps, w_out_b, g1, n2, sh2, sc2, router_w_b, rb)
        route = route.reshape(N, LANES)
        block_e, first, nused, wslot, next_e, tok3, dst3, n_rows = _slots(route, counts[0], N)
        ybuf = _experts(l, h2t, exp_w1, b1, exp_w2, b2, block_e, first, nused, wslot, next_e, tok3, dst3, n_rows,
                        dump_start=N * TOP_K, n_dump=N_EXPERTS)
        x = _combine(ybuf, route, x1.reshape(N, D), g2, S).reshape(B, S, D)
    return x
```
